```python
import jax, jax.numpy as jnp
from jax import lax
import numpy as np

D_MODEL = 4096
BATCH = 4
SEQ = 2048
DEPTH = 1
DEC_BATCH = 128
DEC_SEQ = 4
PAST_LEN = 16384
PAGE_SIZE = 128

CONV_WIDTH = D_MODEL // 2
CONV_K = 31
RWKV_WIDTH = D_MODEL // 2
RWKV_HEAD = 64
RWKV_HEADS = RWKV_WIDTH // RWKV_HEAD
DECAY_RANK = 64
ICLR_RANK = 64
GATE_RANK = 128
SHIFT_W = 3 * RWKV_WIDTH + DECAY_RANK + ICLR_RANK + GATE_RANK
P_TOTAL = 2 * CONV_WIDTH + SHIFT_W + 2 * D_MODEL
D_FF = -(-8 * D_MODEL // 768) * 256
RMS_EPS = 1e-6
LN_EPS = 1e-5
GN_EPS = 64e-5

kernel_name = 'hybrid_conformer_rwkv7_gated_decode_step'


def rms_norm(x, g):
    xf = x.astype(jnp.float32)
    y = xf * lax.rsqrt(jnp.mean(xf * xf, axis=-1, keepdims=True) + RMS_EPS)
    return (y * g.astype(jnp.float32)).astype(x.dtype)


def layer_norm(x, g, b):
    xf = x.astype(jnp.float32)
    mu = jnp.mean(xf, axis=-1, keepdims=True)
    var = jnp.mean(jnp.square(xf - mu), axis=-1, keepdims=True)
    y = (xf - mu) * lax.rsqrt(var + LN_EPS)
    return (y * g.astype(jnp.float32) + b.astype(jnp.float32)).astype(x.dtype)


def conformer_branch(pc, conv_buf, conv_w, conv_b, ln_g, ln_b, w_out):
    u = pc[..., :CONV_WIDTH] * jax.nn.sigmoid(pc[..., CONV_WIDTH:])
    ext = jnp.concatenate([conv_buf.astype(u.dtype), u], axis=1)
    z = lax.conv_general_dilated(
        ext, conv_w[:, None, :].astype(u.dtype), window_strides=(1,), padding='VALID',
        dimension_numbers=('NWC', 'WIO', 'NWC'), feature_group_count=CONV_WIDTH) + conv_b
    z = jax.nn.silu(layer_norm(z, ln_g, ln_b))
    return z @ w_out, ext[:, ext.shape[1] - (CONV_K - 1):]


def wkv_scan(S0, r, w, k, v, a_vec, b_vec):
    def step(S, inp):
        r_t, w_t, k_t, v_t, a_t, b_t = inp
        Sa = jnp.einsum('bhvk,bhk->bhv', S, a_t)
        S = (S * w_t[:, :, None, :] + Sa[..., None] * b_t[:, :, None, :]
             + v_t[..., :, None] * k_t[:, :, None, :])
        y = jnp.einsum('bhvk,bhk->bhv', S, r_t)
        return S, y
    xs = (jnp.swapaxes(r, 0, 1), jnp.swapaxes(w, 0, 1), jnp.swapaxes(k, 0, 1),
          jnp.swapaxes(v, 0, 1), jnp.swapaxes(a_vec, 0, 1), jnp.swapaxes(b_vec, 0, 1))
    S, ys = lax.scan(step, S0, xs)
    return jnp.swapaxes(ys, 0, 1), S


def rwkv7_branch(pr, shift_prev, S0, mu, w0, w2, a0, a2, g2, k_k, k_a, r_k, gn_g, gn_b, w_out):
    B, T, _ = pr.shape
    f32 = jnp.float32
    shifted = jnp.concatenate([shift_prev[:, None].astype(pr.dtype), pr[:, :-1]], axis=1)
    m = pr + (shifted - pr) * mu
    D = RWKV_WIDTH
    r, k, v = m[..., :D], m[..., D:2 * D], m[..., 2 * D:3 * D]
    o = 3 * D
    wl = m[..., o:o + DECAY_RANK]
    o += DECAY_RANK
    al = m[..., o:o + ICLR_RANK]
    o += ICLR_RANK
    gl = m[..., o:o + GATE_RANK]
    w_log = -jax.nn.softplus(-(w0 + jnp.tanh(wl) @ w2)) - 0.5
    decay = jnp.exp(-jnp.exp(w_log.astype(f32)))
    a = jax.nn.sigmoid(a0 + al @ a2)
    g = jax.nn.sigmoid(gl) @ g2

    def hs(t):
        return t.reshape(B, T, RWKV_HEADS, RWKV_HEAD).astype(f32)

    kk = hs(k * k_k)
    kk = kk / jnp.maximum(jnp.sqrt(jnp.sum(kk * kk, axis=-1, keepdims=True)), 1e-12)
    k = k * (1.0 + (a - 1.0) * k_a)
    rh, kh, vh, ah = hs(r), hs(k), hs(v), hs(a)
    y, S = wkv_scan(S0.astype(f32), rh, hs(decay), kh, vh, -kk, kk * ah)
    ym = jnp.mean(y, axis=-1, keepdims=True)
    yv = jnp.mean(jnp.square(y - ym), axis=-1, keepdims=True)
    y = ((y - ym) * lax.rsqrt(yv + GN_EPS)).reshape(B, T, D) * gn_g.astype(f32) + gn_b.astype(f32)
    bonus = jnp.sum(rh * kh * r_k.astype(f32), axis=-1, keepdims=True) * vh
    y = ((y + bonus.reshape(B, T, D)) * g.astype(f32)).astype(pr.dtype)
    return y @ w_out, S, pr[:, -1]


def decoder_layer(x, wkv0, conv0, shift0,
                  ln_mix_pre, ln_mix_post, ln_ffn_pre, ln_ffn_post, w_in, b_gate,
                  conv_w, conv_b, conv_ln_g, conv_ln_b, w_conv_out, shift_mu,
                  w0, w2, a0, a2, g2, k_k, k_a, r_k, gn_g, gn_b, w_rwkv_out, w_o,
                  w_ffn_gate, w_ffn_up, w_ffn_down):
    B, T, _ = x.shape
    h = rms_norm(x, ln_mix_pre)
    p = h @ w_in
    c_end = 2 * CONV_WIDTH
    r_end = c_end + SHIFT_W
    out_c, conv_new = conformer_branch(p[..., :c_end], conv0, conv_w, conv_b,
                                       conv_ln_g, conv_ln_b, w_conv_out)
    out_r, wkv_new, shift_new = rwkv7_branch(p[..., c_end:r_end], shift0, wkv0, shift_mu,
                                             w0, w2, a0, a2, g2, k_k, k_a, r_k,
                                             gn_g, gn_b, w_rwkv_out)
    gates = jax.nn.sigmoid(p[..., r_end:].reshape(B, T, 2, D_MODEL) + b_gate)
    mixed = gates[:, :, 0] * out_c + gates[:, :, 1] * out_r
    x = x + rms_norm(mixed @ w_o, ln_mix_post)
    h2 = rms_norm(x, ln_ffn_pre)
    f = (jax.nn.silu(h2 @ w_ffn_gate) * (h2 @ w_ffn_up)) @ w_ffn_down
    x = x + rms_norm(f, ln_ffn_post)
    return x, wkv_new, conv_new, shift_new


def setup_inputs(seed: int = 0) -> dict:
    key = jax.random.key(seed)
    ks = list(jax.random.split(key, 40))

    def nrm(i, shape, s):
        return s * jax.random.normal(ks[i], shape, jnp.float32)

    L = DEPTH
    x_prompt = nrm(0, (BATCH, SEQ, D_MODEL), 1.0)
    x_sample = nrm(1, (DEC_BATCH, DEC_SEQ, D_MODEL), 1.0)
    state_wkv = nrm(2, (L, DEC_BATCH, RWKV_HEADS, RWKV_HEAD, RWKV_HEAD), 0.3)
    state_conv = nrm(3, (L, DEC_BATCH, CONV_K - 1, CONV_WIDTH), 0.5)
    state_shift = nrm(4, (L, DEC_BATCH, SHIFT_W), 1.0)
    ln_mix_pre = 1.0 + nrm(5, (L, D_MODEL), 0.01)
    ln_mix_post = 1.0 + nrm(6, (L, D_MODEL), 0.01)
    ln_ffn_pre = 1.0 + nrm(7, (L, D_MODEL), 0.01)
    ln_ffn_post = 1.0 + nrm(8, (L, D_MODEL), 0.01)
    w_in = nrm(9, (L, D_MODEL, P_TOTAL), D_MODEL ** -0.5)
    b_gate = nrm(10, (L, 2, D_MODEL), 0.1)
    conv_w = nrm(11, (L, CONV_K, CONV_WIDTH), CONV_K ** -0.5)
    conv_b = nrm(12, (L, CONV_WIDTH), 0.01)
    conv_ln_g = 1.0 + nrm(13, (L, CONV_WIDTH), 0.01)
    conv_ln_b = nrm(14, (L, CONV_WIDTH), 0.01)
    w_conv_out = nrm(15, (L, CONV_WIDTH, D_MODEL), CONV_WIDTH ** -0.5)
    shift_mu = jax.random.uniform(ks[16], (L, SHIFT_W), jnp.float32, 0.0, 1.0)
    w0 = jax.random.uniform(ks[17], (L, RWKV_WIDTH), jnp.float32, -5.0, -0.5)
    w2 = nrm(18, (L, DECAY_RANK, RWKV_WIDTH), 0.5 * DECAY_RANK ** -0.5)
    a0 = nrm(19, (L, RWKV_WIDTH), 0.1)
    a2 = nrm(20, (L, ICLR_RANK, RWKV_WIDTH), 0.5 * ICLR_RANK ** -0.5)
    g2 = nrm(21, (L, GATE_RANK, RWKV_WIDTH), GATE_RANK ** -0.5)
    k_k = 0.85 + nrm(22, (L, RWKV_WIDTH), 0.02)
    k_a = 1.0 + nrm(23, (L, RWKV_WIDTH), 0.02)
    r_k = nrm(24, (L, RWKV_HEADS, RWKV_HEAD), 0.1)
    gn_g = 1.0 + nrm(25, (L, RWKV_WIDTH), 0.01)
    gn_b = nrm(26, (L, RWKV_WIDTH), 0.01)
    w_rwkv_out = nrm(27, (L, RWKV_WIDTH, D_MODEL), RWKV_WIDTH ** -0.5)
    w_o = nrm(28, (L, D_MODEL, D_MODEL), D_MODEL ** -0.5)
    w_ffn_gate = nrm(29, (L, D_MODEL, D_FF), D_MODEL ** -0.5)
    w_ffn_up = nrm(30, (L, D_MODEL, D_FF), D_MODEL ** -0.5)
    w_ffn_down = nrm(31, (L, D_FF, D_MODEL), D_FF ** -0.5)
    return {'x_prompt': x_prompt, 'x_sample': x_sample,
            'state_wkv': state_wkv, 'state_conv': state_conv, 'state_shift': state_shift,
            'ln_mix_pre': ln_mix_pre, 'ln_mix_post': ln_mix_post,
            'ln_ffn_pre': ln_ffn_pre, 'ln_ffn_post': ln_ffn_post,
            'w_in': w_in, 'b_gate': b_gate, 'conv_w': conv_w, 'conv_b': conv_b,
            'conv_ln_g': conv_ln_g, 'conv_ln_b': conv_ln_b, 'w_conv_out': w_conv_out,
            'shift_mu': shift_mu, 'w0': w0, 'w2': w2, 'a0': a0, 'a2': a2, 'g2': g2,
            'k_k': k_k, 'k_a': k_a, 'r_k': r_k, 'gn_g': gn_g, 'gn_b': gn_b,
            'w_rwkv_out': w_rwkv_out, 'w_o': w_o,
            'w_ffn_gate': w_ffn_gate, 'w_ffn_up': w_ffn_up, 'w_ffn_down': w_ffn_down}


def reference(x_prompt, x_sample, state_wkv, state_conv, state_shift,
              ln_mix_pre, ln_mix_post, ln_ffn_pre, ln_ffn_post, w_in, b_gate,
              conv_w, conv_b, conv_ln_g, conv_ln_b, w_conv_out, shift_mu,
              w0, w2, a0, a2, g2, k_k, k_a, r_k, gn_g, gn_b, w_rwkv_out, w_o,
              w_ffn_gate, w_ffn_up, w_ffn_down):
    weights = (ln_mix_pre, ln_mix_post, ln_ffn_pre, ln_ffn_post, w_in, b_gate,
               conv_w, conv_b, conv_ln_g, conv_ln_b, w_conv_out, shift_mu,
               w0, w2, a0, a2, g2, k_k, k_a, r_k, gn_g, gn_b, w_rwkv_out, w_o,
               w_ffn_gate, w_ffn_up, w_ffn_down)
    bp = x_prompt.shape[0]
    yp, ys = x_prompt, x_sample
    wkv_p, conv_p, shift_p, wkv_s, conv_s, shift_s = [], [], [], [], [], []
    for l in range(DEPTH):
        lw = tuple(w[l] for w in weights)
        yp, s_wkv, s_conv, s_shift = decoder_layer(
            yp,
            jnp.zeros((bp, RWKV_HEADS, RWKV_HEAD, RWKV_HEAD), jnp.float32),
            jnp.zeros((bp, CONV_K - 1, CONV_WIDTH), x_prompt.dtype),
            jnp.zeros((bp, SHIFT_W), x_prompt.dtype), *lw)
        wkv_p.append(s_wkv.astype(x_prompt.dtype))
        conv_p.append(s_conv)
        shift_p.append(s_shift)
        ys, d_wkv, d_conv, d_shift = decoder_layer(
            ys, state_wkv[l], state_conv[l], state_shift[l], *lw)
        wkv_s.append(d_wkv.astype(state_wkv.dtype))
        conv_s.append(d_conv.astype(state_conv.dtype))
        shift_s.append(d_shift.astype(state_shift.dtype))
    return (yp, ys,
            jnp.stack(wkv_p), jnp.stack(conv_p), jnp.stack(shift_p),
            jnp.stack(wkv_s), jnp.stack(conv_s), jnp.stack(shift_s))
```

```python
import functools
import math

import jax
import jax.numpy as jnp
from jax import lax
from jax.experimental import pallas as pl
from jax.experimental.pallas import tpu as pltpu

F32 = jnp.float32
BF16 = jnp.bfloat16

RMS_EPS = 1e-6
LN_EPS = 1e-5
GN_EPS = 64e-5
HEAD = 64
LANES = 128
SUBLANES = 8
CONV_HALO = 32
VMEM_LIMIT = 56 * 1024 * 1024


def _pick(n, cands):
    for c in cands:
        if n % c == 0:
            return c
    return n


def _params(sem):
    return pltpu.CompilerParams(dimension_semantics=sem, vmem_limit_bytes=VMEM_LIMIT)


def _dot(a, b):
    return jnp.dot(a, b, preferred_element_type=F32)


def _dot_split(x, w):
    hi = x.astype(BF16)
    lo = (x - hi.astype(F32)).astype(BF16)
    return _dot(hi, w) + _dot(lo, w)


def _head_sum(x, bd):
    cols = [_dot(x[:, c:c + LANES].astype(BF16), bd) for c in range(0, x.shape[1], LANES)]
    return jnp.concatenate(cols, axis=1)


def _rms_cast_kernel(x_ref, g_ref, o_ref):
    x = x_ref[...]
    ms = jnp.mean(x * x, axis=-1, keepdims=True)
    o_ref[...] = (x * lax.rsqrt(ms + RMS_EPS) * g_ref[...]).astype(o_ref.dtype)


def _rms_cast(x, g):
    m, d = x.shape
    tm = _pick(m, (256, 128, 64, 32, 16, 8))
    return pl.pallas_call(
        _rms_cast_kernel,
        grid=(m // tm,),
        in_specs=[pl.BlockSpec((tm, d), lambda i: (i, 0)),
                  pl.BlockSpec((1, d), lambda i: (0, 0))],
        out_specs=pl.BlockSpec((tm, d), lambda i: (i, 0)),
        out_shape=jax.ShapeDtypeStruct((m, d), BF16),
        compiler_params=_params(("parallel",)),
        name="rms_cast",
    )(x, g)


def _rms_res_kernel(x_ref, o_ref, g_ref, g2_ref, x1_ref, h_ref):
    o = o_ref[...]
    ms = jnp.mean(o * o, axis=-1, keepdims=True)
    x1 = x_ref[...] + o * lax.rsqrt(ms + RMS_EPS) * g_ref[...]
    x1_ref[...] = x1
    ms1 = jnp.mean(x1 * x1, axis=-1, keepdims=True)
    h_ref[...] = (x1 * lax.rsqrt(ms1 + RMS_EPS) * g2_ref[...]).astype(h_ref.dtype)


def _rms_res(x, o, g, g2):
    m, d = x.shape
    tm = _pick(m, (256, 128, 64, 32, 16, 8))
    row = pl.BlockSpec((tm, d), lambda i: (i, 0))
    vec = pl.BlockSpec((1, d), lambda i: (0, 0))
    return pl.pallas_call(
        _rms_res_kernel,
        grid=(m // tm,),
        in_specs=[row, row, vec, vec],
        out_specs=[row, row],
        out_shape=[jax.ShapeDtypeStruct((m, d), F32), jax.ShapeDtypeStruct((m, d), BF16)],
        compiler_params=_params(("parallel",)),
        name="rms_res",
    )(x, o, g, g2)


def _rms_res_last_kernel(x_ref, o_ref, g_ref, x1_ref):
    o = o_ref[...]
    ms = jnp.mean(o * o, axis=-1, keepdims=True)
    x1_ref[...] = x_ref[...] + o * lax.rsqrt(ms + RMS_EPS) * g_ref[...]


def _rms_res_last(x, o, g):
    m, d = x.shape
    tm = _pick(m, (256, 128, 64, 32, 16, 8))
    row = pl.BlockSpec((tm, d), lambda i: (i, 0))
    vec = pl.BlockSpec((1, d), lambda i: (0, 0))
    return pl.pallas_call(
        _rms_res_last_kernel,
        grid=(m // tm,),
        in_specs=[row, row, vec],
        out_specs=row,
        out_shape=jax.ShapeDtypeStruct((m, d), F32),
        compiler_params=_params(("parallel",)),
        name="rms_res_last",
    )(x, o, g)


def _mm_kernel(x_ref, w_ref, o_ref):
    o_ref[...] = _dot(x_ref[...], w_ref[...]).astype(o_ref.dtype)


def _mm(x, w, out_dtype=F32, name="mm"):
    m, k = x.shape
    n = w.shape[1]
    tm = _pick(m, (512, 256, 128, 64, 32, 16))
    tn = _pick(n, (512, 256, 128))
    return pl.pallas_call(
        _mm_kernel,
        grid=(m // tm, n // tn),
        in_specs=[pl.BlockSpec((tm, k), lambda i, j: (i, 0)),
                  pl.BlockSpec((k, tn), lambda i, j: (0, j))],
        out_specs=pl.BlockSpec((tm, tn), lambda i, j: (i, j)),
        out_shape=jax.ShapeDtypeStruct((m, n), out_dtype),
        compiler_params=_params(("parallel", "arbitrary")),
        name=name,
    )(x, w)


def _mm_glu_kernel(x_ref, wa_ref, wb_ref, o_ref):
    x = x_ref[...]
    o_ref[...] = _dot(x, wa_ref[...]) * jax.nn.sigmoid(_dot(x, wb_ref[...]))


def _mm_glu(x, wa, wb):
    m, k = x.shape
    n = wa.shape[1]
    tm = _pick(m, (512, 256, 128, 64, 32, 16))
    tn = _pick(n, (256, 128))
    wspec = pl.BlockSpec((k, tn), lambda i, j: (0, j))
    return pl.pallas_call(
        _mm_glu_kernel,
        grid=(m // tm, n // tn),
        in_specs=[pl.BlockSpec((tm, k), lambda i, j: (i, 0)), wspec, wspec],
        out_specs=pl.BlockSpec((tm, tn), lambda i, j: (i, j)),
        out_shape=jax.ShapeDtypeStruct((m, n), F32),
        compiler_params=_params(("parallel", "arbitrary")),
        name="mm_glu",
    )(x, wa, wb)


def _mm_swiglu_kernel(x_ref, wg_ref, wu_ref, o_ref):
    x = x_ref[...]
    o_ref[...] = (jax.nn.silu(_dot(x, wg_ref[...])) * _dot(x, wu_ref[...])).astype(o_ref.dtype)


def _mm_swiglu(x, wg, wu):
    m, k = x.shape
    n = wg.shape[1]
    tm = _pick(m, (512, 256, 128, 64, 32, 16))
    tn = _pick(n, (256, 128))
    wspec = pl.BlockSpec((k, tn), lambda i, j: (0, j))
    return pl.pallas_call(
        _mm_swiglu_kernel,
        grid=(m // tm, n // tn),
        in_specs=[pl.BlockSpec((tm, k), lambda i, j: (i, 0)), wspec, wspec],
        out_specs=pl.BlockSpec((tm, tn), lambda i, j: (i, j)),
        out_shape=jax.ShapeDtypeStruct((m, n), BF16),
        compiler_params=_params(("parallel", "arbitrary")),
        name="mm_swiglu",
    )(x, wg, wu)


def _merge_kernel(h_ref, zc_ref, yr_ref, wg0_ref, wg1_ref, wc_ref, wr_ref, b0_ref, b1_ref, o_ref):
    h = h_ref[...]
    g0 = jax.nn.sigmoid(_dot(h, wg0_ref[...]) + b0_ref[...])
    g1 = jax.nn.sigmoid(_dot(h, wg1_ref[...]) + b1_ref[...])
    oc = _dot(zc_ref[...], wc_ref[...])
    orr = _dot(yr_ref[...], wr_ref[...])
    o_ref[...] = (g0 * oc + g1 * orr).astype(o_ref.dtype)


def _merge(h, zc, yr, wg0, wg1, wc, wr, b0, b1):
    m, d = h.shape
    c = zc.shape[1]
    tm = _pick(m, (512, 256, 128, 64, 32, 16))
    tn = _pick(d, (256, 128))
    return pl.pallas_call(
        _merge_kernel,
        grid=(m // tm, d // tn),
        in_specs=[pl.BlockSpec((tm, d), lambda i, j: (i, 0)),
                  pl.BlockSpec((tm, c), lambda i, j: (i, 0)),
                  pl.BlockSpec((tm, c), lambda i, j: (i, 0)),
                  pl.BlockSpec((d, tn), lambda i, j: (0, j)),
                  pl.BlockSpec((d, tn), lambda i, j: (0, j)),
                  pl.BlockSpec((c, tn), lambda i, j: (0, j)),
                  pl.BlockSpec((c, tn), lambda i, j: (0, j)),
                  pl.BlockSpec((1, tn), lambda i, j: (0, j)),
                  pl.BlockSpec((1, tn), lambda i, j: (0, j))],
        out_specs=pl.BlockSpec((tm, tn), lambda i, j: (i, j)),
        out_shape=jax.ShapeDtypeStruct((m, d), BF16),
        compiler_params=_params(("parallel", "arbitrary")),
        name="merge",
    )(h, zc, yr, wg0, wg1, wc, wr, b0, b1)


def _ln_silu(z, g, b):
    mu = jnp.mean(z, axis=-1, keepdims=True)
    zc = z - mu
    var = jnp.mean(zc * zc, axis=-1, keepdims=True)
    y = zc * lax.rsqrt(var + LN_EPS) * g + b
    return y * jax.nn.sigmoid(y)


def _conv_seq_kernel(um_ref, uh_ref, w_ref, cb_ref, g_ref, b_ref, o_ref, x_scr, z_scr, *, tb, kw, lc):
    i = pl.program_id(1)
    c = um_ref.shape[2]
    halo = uh_ref[0]
    x_scr[0:CONV_HALO, :] = jnp.where(i == 0, jnp.zeros_like(halo), halo)
    x_scr[CONV_HALO:CONV_HALO + tb, :] = um_ref[0]
    off = CONV_HALO - (kw - 1)
    for c0 in range(0, c, lc):
        acc = jnp.zeros((tb, lc), F32)
        for j in range(kw):
            acc = acc + x_scr[off + j:off + j + tb, c0:c0 + lc] * w_ref[j:j + 1, c0:c0 + lc]
        z_scr[:, c0:c0 + lc] = acc + cb_ref[:, c0:c0 + lc]
    o_ref[0] = _ln_silu(z_scr[...], g_ref[...], b_ref[...]).astype(o_ref.dtype)


def _conv_seq(u, conv_w, conv_b, ln_g, ln_b):
    b, t, c = u.shape
    kw = conv_w.shape[0]
    tb = _pick(t, (32,))
    lc = _pick(c, (512, 256, 128))
    hb = tb // CONV_HALO
    vec = pl.BlockSpec((1, c), lambda bi, i: (0, 0))
    return pl.pallas_call(
        functools.partial(_conv_seq_kernel, tb=tb, kw=kw, lc=lc),
        grid=(b, t // tb),
        in_specs=[pl.BlockSpec((1, tb, c), lambda bi, i: (bi, i, 0)),
                  pl.BlockSpec((1, CONV_HALO, c), lambda bi, i: (bi, jnp.maximum(i * hb - 1, 0), 0)),
                  pl.BlockSpec((kw, c), lambda bi, i: (0, 0)),
                  vec, vec, vec],
        out_specs=pl.BlockSpec((1, tb, c), lambda bi, i: (bi, i, 0)),
        out_shape=jax.ShapeDtypeStruct((b, t, c), BF16),
        scratch_shapes=[pltpu.VMEM((CONV_HALO + tb, c), F32), pltpu.VMEM((tb, c), F32)],
        compiler_params=_params(("parallel", "arbitrary")),
        name="conv_seq",
    )(u, u, conv_w, conv_b, ln_g, ln_b)


def _conv_step_kernel(e_ref, w_ref, cb_ref, g_ref, b_ref, o_ref, *, kw, nt):
    for t in range(nt):
        acc = e_ref[t] * w_ref[0:1, :]
        for j in range(1, kw):
            acc = acc + e_ref[t + j] * w_ref[j:j + 1, :]
        o_ref[t] = _ln_silu(acc + cb_ref[...], g_ref[...], b_ref[...]).astype(o_ref.dtype)


def _conv_step(ext_tm, conv_w, conv_b, ln_g, ln_b):
    te, b, c = ext_tm.shape
    kw = conv_w.shape[0]
    nt = te - (kw - 1)
    sb = _pick(b, (8,))
    vec = pl.BlockSpec((1, c), lambda i: (0, 0))
    return pl.pallas_call(
        functools.partial(_conv_step_kernel, kw=kw, nt=nt),
        grid=(b // sb,),
        in_specs=[pl.BlockSpec((te, sb, c), lambda i: (0, i, 0)),
                  pl.BlockSpec((kw, c), lambda i: (0, 0)),
                  vec, vec, vec],
        out_specs=pl.BlockSpec((nt, sb, c), lambda i: (0, i, 0)),
        out_shape=jax.ShapeDtypeStruct((nt, b, c), BF16),
        compiler_params=_params(("parallel",)),
        name="conv_step",
    )(ext_tm, conv_w, conv_b, ln_g, ln_b)


def _rwkv_prep_kernel(pr_ref, sh_ref, mu_ref, w0_ref, a0_ref, wda_ref, g2_ref, kk_ref, ka_ref, rk_ref,
                      bd_ref, r_o, w_o, k_o, v_o, a_o, b_o, g_o, bn_o, *, c, dr):
    pr = pr_ref[...]
    m = pr + (sh_ref[...] - pr) * mu_ref[...]
    r = m[:, 0:c]
    k = m[:, c:2 * c]
    v = m[:, 2 * c:3 * c]
    low = m[:, 3 * c:3 * c + 2 * dr]
    lane = lax.broadcasted_iota(jnp.int32, low.shape, 1)
    low = jnp.where(lane < dr, jnp.tanh(low), low)
    dd = _dot(low.astype(BF16), wda_ref[...])
    logw = -math.exp(-0.5) * jax.nn.sigmoid(w0_ref[...] + dd[:, 0:c])
    a = jax.nn.sigmoid(a0_ref[...] + dd[:, c:2 * c])
    g = _dot(jax.nn.sigmoid(m[:, 3 * c + 2 * dr:]).astype(BF16), g2_ref[...])
    bd = bd_ref[...]
    kk = k * kk_ref[...]
    nrm = jnp.maximum(jnp.sqrt(_head_sum(kk * kk, bd)), 1e-12)
    kk = kk / nrm
    k2 = k * (1.0 + (a - 1.0) * ka_ref[...])
    r_o[...] = r
    w_o[...] = jnp.exp(logw)
    k_o[...] = k2
    v_o[...] = v
    a_o[...] = -kk
    b_o[...] = kk * a
    g_o[...] = g
    bn_o[...] = _head_sum(r * k2 * rk_ref[...], bd)


def _rwkv_prep(pr, shifted, mu, w0, a0, wda, g2, k_k, k_a, r_k, bd, c, dr):
    m, sw = pr.shape
    tm = _pick(m, (128, 64, 32, 16, 8))
    row = pl.BlockSpec((tm, sw), lambda i: (i, 0))
    out = pl.BlockSpec((tm, c), lambda i: (i, 0))
    cvec = pl.BlockSpec((1, c), lambda i: (0, 0))
    full = lambda a: pl.BlockSpec(a.shape, lambda i: (0,) * a.ndim)
    return pl.pallas_call(
        functools.partial(_rwkv_prep_kernel, c=c, dr=dr),
        grid=(m // tm,),
        in_specs=[row, row, full(mu), cvec, cvec, full(wda), full(g2), cvec, cvec, cvec, full(bd)],
        out_specs=[out] * 8,
        out_shape=[jax.ShapeDtypeStruct((m, c), F32)] * 8,
        compiler_params=_params(("parallel",)),
        name="rwkv_prep",
    )(pr, shifted, mu, w0, a0, wda, g2, k_k, k_a, r_k, bd)


def _wkv_seq_kernel(r_ref, w_ref, k_ref, v_ref, a_ref, b_ref, s0_ref, bd_ref, y_ref, st_ref, s_scr,
                    *, tt, g, sub):
    ti = pl.program_id(2)

    @pl.when(ti == 0)
    def _():
        s_scr[...] = s0_ref[0]

    bd = bd_ref[...]
    rows = lax.broadcasted_iota(jnp.int32, (HEAD, LANES), 0)
    lanes = lax.broadcasted_iota(jnp.int32, (HEAD, LANES), 1)
    diag = jnp.where(rows == lanes % HEAD, 1.0, 0.0).astype(F32)
    subrow = lax.broadcasted_iota(jnp.int32, (sub, LANES), 0)

    def block(bi, carry):
        t0 = pl.multiple_of(bi * sub, sub)
        rr = r_ref[0, pl.ds(t0, sub), :]
        ww = w_ref[0, pl.ds(t0, sub), :]
        kk = k_ref[0, pl.ds(t0, sub), :]
        vv = v_ref[0, pl.ds(t0, sub), :]
        aa = a_ref[0, pl.ds(t0, sub), :]
        bb = b_ref[0, pl.ds(t0, sub), :]
        for p in range(g):
            sl = slice(p * LANES, (p + 1) * LANES)
            s = s_scr[p]
            ytile = jnp.zeros((sub, LANES), F32)
            for i in range(sub):
                bc = lambda x: jnp.broadcast_to(x[i:i + 1, sl], (HEAD, LANES))
                lhs = jnp.concatenate([s * bc(aa), bc(vv) * diag], axis=0)
                res = _dot_split(lhs, bd)
                s = s * bc(ww) + res[0:HEAD] * bc(bb) + res[HEAD:2 * HEAD] * bc(kk)
                yc = _dot_split(s * bc(rr), bd)
                yrow = jnp.sum(yc * diag, axis=0, keepdims=True)
                ytile = jnp.where(subrow == i, jnp.broadcast_to(yrow, (sub, LANES)), ytile)
            s_scr[p] = s
            y_ref[0, pl.ds(t0, sub), sl] = ytile
        return carry

    lax.fori_loop(0, tt // sub, block, 0)

    @pl.when(ti == pl.num_programs(2) - 1)
    def _():
        st_ref[0] = s_scr[...]


def _wkv_seq(r, w, k, v, a, b, s0, bd):
    s, t, c = r.shape
    npair = c // LANES
    g = _pick(npair, (4, 2, 1))
    tt = _pick(t, (256, 128, 64, 32, 16, 8))
    sub = min(SUBLANES, tt)
    seq = pl.BlockSpec((1, tt, g * LANES), lambda si, pi, ti: (si, ti, pi))
    st = pl.BlockSpec((1, g, HEAD, LANES), lambda si, pi, ti: (si, pi, 0, 0))
    return pl.pallas_call(
        functools.partial(_wkv_seq_kernel, tt=tt, g=g, sub=sub),
        grid=(s, npair // g, t // tt),
        in_specs=[seq] * 6 + [st, pl.BlockSpec((LANES, LANES), lambda si, pi, ti: (0, 0))],
        out_specs=[seq, st],
        out_shape=[jax.ShapeDtypeStruct((s, t, c), F32),
                   jax.ShapeDtypeStruct((s, npair, HEAD, LANES), F32)],
        scratch_shapes=[pltpu.VMEM((g, HEAD, LANES), F32)],
        compiler_params=_params(("parallel", "parallel", "arbitrary")),
        name="wkv_seq",
    )(r, w, k, v, a, b, s0, bd)


def _rwkv_post_kernel(y_ref, bn_ref, v_ref, g_ref, gg_ref, gb_ref, bd_ref, o_ref):
    y = y_ref[...]
    bd = bd_ref[...]
    ym = _head_sum(y, bd) * (1.0 / HEAD)
    yc = y - ym
    yv = _head_sum(yc * yc, bd) * (1.0 / HEAD)
    yn = yc * lax.rsqrt(yv + GN_EPS) * gg_ref[...] + gb_ref[...]
    o_ref[...] = ((yn + bn_ref[...] * v_ref[...]) * g_ref[...]).astype(o_ref.dtype)


def _rwkv_post(y, bn, v, g, gn_g, gn_b, bd):
    m, c = y.shape
    tm = _pick(m, (256, 128, 64, 32, 16, 8))
    row = pl.BlockSpec((tm, c), lambda i: (i, 0))
    vec = pl.BlockSpec((1, c), lambda i: (0, 0))
    return pl.pallas_call(
        _rwkv_post_kernel,
        grid=(m // tm,),
        in_specs=[row, row, row, row, vec, vec, pl.BlockSpec((LANES, LANES), lambda i: (0, 0))],
        out_specs=row,
        out_shape=jax.ShapeDtypeStruct((m, c), BF16),
        compiler_params=_params(("parallel",)),
        name="rwkv_post",
    )(y, bn, v, g, gn_g, gn_b, bd)


def _pack_state(s):
    n, h = s.shape[0], s.shape[1]
    return s.reshape(n, h // 2, 2, HEAD, HEAD).transpose(0, 1, 3, 2, 4).reshape(n, h // 2, HEAD, 2 * HEAD)


def _unpack_state(s):
    n, hp = s.shape[0], s.shape[1]
    return s.reshape(n, hp, HEAD, 2, HEAD).transpose(0, 1, 3, 2, 4).reshape(n, 2 * hp, HEAD, HEAD)


def _layer(x_prompt, x_sample, wkv0, conv0, shift0,
           ln_mix_pre, ln_mix_post, ln_ffn_pre, ln_ffn_post, w_in, b_gate,
           conv_w, conv_b, conv_ln_g, conv_ln_b, w_conv_out, shift_mu,
           w0, w2, a0, a2, g2, k_k, k_a, r_k, gn_g, gn_b, w_rwkv_out, w_o,
           w_ffn_gate, w_ffn_up, w_ffn_down):
    bp, tp, d = x_prompt.shape
    bs, ts, _ = x_sample.shape
    c = conv_w.shape[1]
    kw = conv_w.shape[0]
    dr = w2.shape[0]
    sw = shift_mu.shape[0]
    mp = bp * tp
    row = lambda v: v.reshape(1, -1)

    w_glu_a = w_in[:, 0:c].astype(BF16)
    w_glu_b = w_in[:, c:2 * c].astype(BF16)
    w_shift = w_in[:, 2 * c:2 * c + sw].astype(BF16)
    w_g0 = w_in[:, 2 * c + sw:2 * c + sw + d].astype(BF16)
    w_g1 = w_in[:, 2 * c + sw + d:].astype(BF16)
    zeros = jnp.zeros((dr, c), F32)
    wda = jnp.concatenate([jnp.concatenate([w2, zeros], axis=1),
                           jnp.concatenate([zeros, a2], axis=1)], axis=0).astype(BF16)
    idx = jnp.arange(LANES) // HEAD
    bd = (idx[:, None] == idx[None, :]).astype(BF16)

    x = jnp.concatenate([x_prompt.reshape(mp, d), x_sample.reshape(bs * ts, d)], axis=0)
    h = _rms_cast(x, row(ln_mix_pre))

    u = _mm_glu(h, w_glu_a, w_glu_b)
    u_p = u[:mp].reshape(bp, tp, c)
    u_s = u[mp:].reshape(bs, ts, c)
    zc_p = _conv_seq(u_p, conv_w, row(conv_b), row(conv_ln_g), row(conv_ln_b))
    ext_s = jnp.concatenate([conv0, u_s], axis=1)
    zc_s = _conv_step(ext_s.transpose(1, 0, 2), conv_w, row(conv_b), row(conv_ln_g), row(conv_ln_b))
    zc = jnp.concatenate([zc_p.reshape(mp, c), zc_s.transpose(1, 0, 2).reshape(bs * ts, c)], axis=0)
    conv_p = u_p[:, tp - (kw - 1):]
    conv_s = ext_s[:, ts:]

    pr = _mm(h, w_shift, name="mm_shift")
    pr_p = pr[:mp].reshape(bp, tp, sw)
    pr_s = pr[mp:].reshape(bs, ts, sw)
    shifted = jnp.concatenate([
        jnp.concatenate([jnp.zeros((bp, 1, sw), F32), pr_p[:, :-1]], axis=1).reshape(mp, sw),
        jnp.concatenate([shift0[:, None], pr_s[:, :-1]], axis=1).reshape(bs * ts, sw)], axis=0)
    r, w, k, v, a, b, g, bn = _rwkv_prep(
        pr, shifted, row(shift_mu), row(w0), row(a0), wda, g2.astype(BF16),
        row(k_k), row(k_a), row(r_k), bd, c, dr)
    seqs_p = [t[:mp].reshape(bp, tp, c) for t in (r, w, k, v, a, b)]
    seqs_s = [t[mp:].reshape(bs, ts, c) for t in (r, w, k, v, a, b)]
    y_p, st_p = _wkv_seq(*seqs_p, jnp.zeros((bp, c // LANES, HEAD, LANES), F32), bd)
    y_s, st_s = _wkv_seq(*seqs_s, _pack_state(wkv0), bd)
    y = jnp.concatenate([y_p.reshape(mp, c), y_s.reshape(bs * ts, c)], axis=0)
    yr = _rwkv_post(y, bn, v, g, row(gn_g), row(gn_b), bd)

    mixed = _merge(h, zc, yr, w_g0, w_g1, w_conv_out.astype(BF16), w_rwkv_out.astype(BF16),
                   b_gate[0:1], b_gate[1:2])
    o = _mm(mixed, w_o.astype(BF16), name="mm_o")
    x1, h2 = _rms_res(x, o, row(ln_mix_post), row(ln_ffn_pre))

    act = _mm_swiglu(h2, w_ffn_gate.astype(BF16), w_ffn_up.astype(BF16))
    f = _mm(act, w_ffn_down.astype(BF16), name="mm_down")
    x2 = _rms_res_last(x1, f, row(ln_ffn_post))

    return (x2[:mp].reshape(bp, tp, d), x2[mp:].reshape(bs, ts, d),
            _unpack_state(st_p), conv_p, pr_p[:, -1],
            _unpack_state(st_s), conv_s, pr_s[:, -1])


def kernel(x_prompt, x_sample, state_wkv, state_conv, state_shift, ln_mix_pre, ln_mix_post, ln_ffn_pre,
           ln_ffn_post, w_in, b_gate, conv_w, conv_b, conv_ln_g, conv_ln_b, w_conv_out, shift_mu, w0, w2,
           a0, a2, g2, k_k, k_a, r_k, gn_g, gn_b, w_rwkv_out, w_o, w_ffn_gate, w_ffn_up, w_ffn_down):
    depth = w_in.shape[0]
    assert depth == 1, "one decoder layer per step"
    weights = (ln_mix_pre, ln_mix_post, ln_ffn_pre, ln_ffn_post, w_in, b_gate,
               conv_w, conv_b, conv_ln_g, conv_ln_b, w_conv_out, shift_mu,
               w0, w2, a0, a2, g2, k_k, k_a, r_k, gn_g, gn_b, w_rwkv_out, w_o,
               w_ffn_gate, w_ffn_up, w_ffn_down)
    lw = tuple(wt[0] for wt in weights)
    yp, ys, wkv_p, conv_p, shift_p, wkv_s, conv_s, shift_s = _layer(
        x_prompt, x_sample, state_wkv[0], state_conv[0], state_shift[0], *lw)
    return (yp, ys, wkv_p[None], conv_p[None], shift_p[None],
            wkv_s[None], conv_s[None], shift_s[None])
```

```python
import functools
import math

import jax
import jax.numpy as jnp
from jax import lax
from jax.experimental import pallas as pl
from jax.experimental.pallas import tpu as pltpu

F32 = jnp.float32
BF16 = jnp.bfloat16

RMS_EPS = 1e-6
LN_EPS = 1e-5
GN_EPS = 64e-5
HEAD = 64
LANES = 128
SUBLANES = 8
CONV_HALO = 32
CHUNK = 64
VMEM_LIMIT = 56 * 1024 * 1024


def _pick(n, cands):
    for c in cands:
        if n % c == 0:
            return c
    return n


def _params(sem):
    return pltpu.CompilerParams(dimension_semantics=sem, vmem_limit_bytes=VMEM_LIMIT)


def _dot(a, b):
    return jnp.dot(a, b, preferred_element_type=F32)


def _split(x):
    hi = x.astype(BF16)
    return hi, (x - hi.astype(F32)).astype(BF16)


def _mm_split_lhs(a, b):
    hi, lo = _split(a)
    bb = b.astype(BF16)
    return _dot(hi, bb) + _dot(lo, bb)


def _mm_split_both(a, b):
    ah, al = _split(a)
    bh, bl = _split(b)
    return _dot(ah, bh) + _dot(al, bh) + _dot(ah, bl)


def _head_sum(x, bd):
    cols = [_dot(x[:, c:c + LANES].astype(BF16), bd) for c in range(0, x.shape[1], LANES)]
    return jnp.concatenate(cols, axis=1)


def _two_group_specs(tm, d, nblk_p):
    return (pl.BlockSpec((tm, d), lambda i, *_: (jnp.minimum(i, nblk_p - 1), 0)),
            pl.BlockSpec((tm, d), lambda i, *_: (jnp.maximum(i - nblk_p, 0), 0)))


def _two_group_rows(p_ref, s_ref, nblk_p):
    return jnp.where(pl.program_id(0) < nblk_p, p_ref[...], s_ref[...])


def _rms_cast_kernel(xp_ref, xs_ref, g_ref, o_ref, *, nblk_p):
    x = _two_group_rows(xp_ref, xs_ref, nblk_p)
    ms = jnp.mean(x * x, axis=-1, keepdims=True)
    o_ref[...] = (x * lax.rsqrt(ms + RMS_EPS) * g_ref[...]).astype(o_ref.dtype)


def _rms_cast(x_p, x_s, g, tm):
    (mp, d), ms = x_p.shape, x_s.shape[0]
    xp_spec, xs_spec = _two_group_specs(tm, d, mp // tm)
    return pl.pallas_call(
        functools.partial(_rms_cast_kernel, nblk_p=mp // tm),
        grid=((mp + ms) // tm,),
        in_specs=[xp_spec, xs_spec, pl.BlockSpec((1, d), lambda i: (0, 0))],
        out_specs=pl.BlockSpec((tm, d), lambda i: (i, 0)),
        out_shape=jax.ShapeDtypeStruct((mp + ms, d), BF16),
        compiler_params=_params(("parallel",)),
        name="rms_cast",
    )(x_p, x_s, g)


def _rms_res_kernel(xp_ref, xs_ref, o_ref, g_ref, g2_ref, x1_ref, h_ref, *, nblk_p):
    o = o_ref[...]
    ms = jnp.mean(o * o, axis=-1, keepdims=True)
    x1 = _two_group_rows(xp_ref, xs_ref, nblk_p) + o * lax.rsqrt(ms + RMS_EPS) * g_ref[...]
    x1_ref[...] = x1
    ms1 = jnp.mean(x1 * x1, axis=-1, keepdims=True)
    h_ref[...] = (x1 * lax.rsqrt(ms1 + RMS_EPS) * g2_ref[...]).astype(h_ref.dtype)


def _rms_res(x_p, x_s, o, g, g2, tm):
    (mp, d), m = x_p.shape, o.shape[0]
    xp_spec, xs_spec = _two_group_specs(tm, d, mp // tm)
    row = pl.BlockSpec((tm, d), lambda i: (i, 0))
    vec = pl.BlockSpec((1, d), lambda i: (0, 0))
    return pl.pallas_call(
        functools.partial(_rms_res_kernel, nblk_p=mp // tm),
        grid=(m // tm,),
        in_specs=[xp_spec, xs_spec, row, vec, vec],
        out_specs=[row, row],
        out_shape=[jax.ShapeDtypeStruct((m, d), F32), jax.ShapeDtypeStruct((m, d), BF16)],
        compiler_params=_params(("parallel",)),
        name="rms_res",
    )(x_p, x_s, o, g, g2)


def _rms_res_last_kernel(x_ref, o_ref, g_ref, x1_ref):
    o = o_ref[...]
    ms = jnp.mean(o * o, axis=-1, keepdims=True)
    x1_ref[...] = x_ref[...] + o * lax.rsqrt(ms + RMS_EPS) * g_ref[...]


def _rms_res_last(x, o, g, row0, rows, tm):
    d = x.shape[1]
    blk0 = row0 // tm
    src = pl.BlockSpec((tm, d), lambda i: (blk0 + i, 0))
    return pl.pallas_call(
        _rms_res_last_kernel,
        grid=(rows // tm,),
        in_specs=[src, src, pl.BlockSpec((1, d), lambda i: (0, 0))],
        out_specs=pl.BlockSpec((tm, d), lambda i: (i, 0)),
        out_shape=jax.ShapeDtypeStruct((rows, d), F32),
        compiler_params=_params(("parallel",)),
        name="rms_res_last",
    )(x, o, g)


def _mm_kernel(x_ref, w_ref, o_ref):
    o_ref[...] = _dot(x_ref[...], w_ref[...]).astype(o_ref.dtype)


def _mm(x, w, out_dtype=F32, name="mm"):
    m, k = x.shape
    n = w.shape[1]
    tm = _pick(m, (512, 256, 128, 64, 32, 16))
    tn = _pick(n, (512, 256, 128))
    return pl.pallas_call(
        _mm_kernel,
        grid=(m // tm, n // tn),
        in_specs=[pl.BlockSpec((tm, k), lambda i, j: (i, 0)),
                  pl.BlockSpec((k, tn), lambda i, j: (0, j))],
        out_specs=pl.BlockSpec((tm, tn), lambda i, j: (i, j)),
        out_shape=jax.ShapeDtypeStruct((m, n), out_dtype),
        compiler_params=_params(("parallel", "arbitrary")),
        name=name,
    )(x, w)


def _mm_glu_kernel(x_ref, wa_ref, wb_ref, o_ref):
    x = x_ref[...]
    o_ref[...] = _dot(x, wa_ref[...]) * jax.nn.sigmoid(_dot(x, wb_ref[...]))


def _mm_glu(x, wa, wb):
    m, k = x.shape
    n = wa.shape[1]
    tm = _pick(m, (512, 256, 128, 64, 32, 16))
    tn = _pick(n, (256, 128))
    wspec = pl.BlockSpec((k, tn), lambda i, j: (0, j))
    return pl.pallas_call(
        _mm_glu_kernel,
        grid=(m // tm, n // tn),
        in_specs=[pl.BlockSpec((tm, k), lambda i, j: (i, 0)), wspec, wspec],
        out_specs=pl.BlockSpec((tm, tn), lambda i, j: (i, j)),
        out_shape=jax.ShapeDtypeStruct((m, n), F32),
        compiler_params=_params(("parallel", "arbitrary")),
        name="mm_glu",
    )(x, wa, wb)


def _mm_swiglu_kernel(x_ref, wg_ref, wu_ref, o_ref):
    x = x_ref[...]
    o_ref[...] = (jax.nn.silu(_dot(x, wg_ref[...])) * _dot(x, wu_ref[...])).astype(o_ref.dtype)


def _mm_swiglu(x, wg, wu):
    m, k = x.shape
    n = wg.shape[1]
    tm = _pick(m, (512, 256, 128, 64, 32, 16))
    tn = _pick(n, (256, 128))
    wspec = pl.BlockSpec((k, tn), lambda i, j: (0, j))
    return pl.pallas_call(
        _mm_swiglu_kernel,
        grid=(m // tm, n // tn),
        in_specs=[pl.BlockSpec((tm, k), lambda i, j: (i, 0)), wspec, wspec],
        out_specs=pl.BlockSpec((tm, tn), lambda i, j: (i, j)),
        out_shape=jax.ShapeDtypeStruct((m, n), BF16),
        compiler_params=_params(("parallel", "arbitrary")),
        name="mm_swiglu",
    )(x, wg, wu)


def _merge_kernel(h_ref, zcp_ref, zcs_ref, yrp_ref, yrs_ref, wg0_ref, wg1_ref, wc_ref, wr_ref,
                  b0_ref, b1_ref, o_ref, *, nblk_p):
    h = h_ref[...]
    g0 = jax.nn.sigmoid(_dot(h, wg0_ref[...]) + b0_ref[...])
    g1 = jax.nn.sigmoid(_dot(h, wg1_ref[...]) + b1_ref[...])
    oc = _dot(_two_group_rows(zcp_ref, zcs_ref, nblk_p), wc_ref[...])
    orr = _dot(_two_group_rows(yrp_ref, yrs_ref, nblk_p), wr_ref[...])
    o_ref[...] = (g0 * oc + g1 * orr).astype(o_ref.dtype)


def _merge(h, zc_p, zc_s, yr_p, yr_s, wg0, wg1, wc, wr, b0, b1, tm):
    m, d = h.shape
    mp, c = zc_p.shape
    tn = _pick(d, (256, 128))
    gp_spec, gs_spec = _two_group_specs(tm, c, mp // tm)
    return pl.pallas_call(
        functools.partial(_merge_kernel, nblk_p=mp // tm),
        grid=(m // tm, d // tn),
        in_specs=[pl.BlockSpec((tm, d), lambda i, j: (i, 0)),
                  gp_spec, gs_spec, gp_spec, gs_spec,
                  pl.BlockSpec((d, tn), lambda i, j: (0, j)),
                  pl.BlockSpec((d, tn), lambda i, j: (0, j)),
                  pl.BlockSpec((c, tn), lambda i, j: (0, j)),
                  pl.BlockSpec((c, tn), lambda i, j: (0, j)),
                  pl.BlockSpec((1, tn), lambda i, j: (0, j)),
                  pl.BlockSpec((1, tn), lambda i, j: (0, j))],
        out_specs=pl.BlockSpec((tm, tn), lambda i, j: (i, j)),
        out_shape=jax.ShapeDtypeStruct((m, d), BF16),
        compiler_params=_params(("parallel", "arbitrary")),
        name="merge",
    )(h, zc_p, zc_s, yr_p, yr_s, wg0, wg1, wc, wr, b0, b1)


def _ln_silu(z, g, b):
    mu = jnp.mean(z, axis=-1, keepdims=True)
    zc = z - mu
    var = jnp.mean(zc * zc, axis=-1, keepdims=True)
    y = zc * lax.rsqrt(var + LN_EPS) * g + b
    return y * jax.nn.sigmoid(y)


def _conv_seq_kernel(um_ref, uh_ref, w_ref, cb_ref, g_ref, b_ref, o_ref, x_scr, z_scr, *, tb, kw, lc):
    i = pl.program_id(1)
    c = um_ref.shape[1]
    halo = uh_ref[...]
    x_scr[0:CONV_HALO, :] = jnp.where(i == 0, jnp.zeros_like(halo), halo)
    x_scr[CONV_HALO:CONV_HALO + tb, :] = um_ref[...]
    off = CONV_HALO - (kw - 1)
    for c0 in range(0, c, lc):
        acc = jnp.zeros((tb, lc), F32)
        for j in range(kw):
            acc = acc + x_scr[off + j:off + j + tb, c0:c0 + lc] * w_ref[j:j + 1, c0:c0 + lc]
        z_scr[:, c0:c0 + lc] = acc + cb_ref[:, c0:c0 + lc]
    o_ref[...] = _ln_silu(z_scr[...], g_ref[...], b_ref[...]).astype(o_ref.dtype)


def _conv_seq(u, nseq, t, conv_w, conv_b, ln_g, ln_b):
    c = u.shape[1]
    kw = conv_w.shape[0]
    tb = _pick(t, (32,))
    lc = _pick(c, (512, 256, 128))
    nb = t // tb
    hb = tb // CONV_HALO
    vec = pl.BlockSpec((1, c), lambda bi, i: (0, 0))
    return pl.pallas_call(
        functools.partial(_conv_seq_kernel, tb=tb, kw=kw, lc=lc),
        grid=(nseq, nb),
        in_specs=[pl.BlockSpec((tb, c), lambda bi, i: (bi * nb + i, 0)),
                  pl.BlockSpec((CONV_HALO, c), lambda bi, i: (jnp.maximum((bi * nb + i) * hb - 1, 0), 0)),
                  pl.BlockSpec((kw, c), lambda bi, i: (0, 0)),
                  vec, vec, vec],
        out_specs=pl.BlockSpec((tb, c), lambda bi, i: (bi * nb + i, 0)),
        out_shape=jax.ShapeDtypeStruct((nseq * t, c), BF16),
        scratch_shapes=[pltpu.VMEM((CONV_HALO + tb, c), F32), pltpu.VMEM((tb, c), F32)],
        compiler_params=_params(("parallel", "arbitrary")),
        name="conv_seq",
    )(u, u, conv_w, conv_b, ln_g, ln_b)


def _conv_step_kernel(e_ref, w_ref, cb_ref, g_ref, b_ref, o_ref, *, kw, nt):
    for t in range(nt):
        acc = e_ref[t] * w_ref[0:1, :]
        for j in range(1, kw):
            acc = acc + e_ref[t + j] * w_ref[j:j + 1, :]
        o_ref[t] = _ln_silu(acc + cb_ref[...], g_ref[...], b_ref[...]).astype(o_ref.dtype)


def _conv_step(ext_tm, conv_w, conv_b, ln_g, ln_b):
    te, b, c = ext_tm.shape
    kw = conv_w.shape[0]
    nt = te - (kw - 1)
    sb = _pick(b, (8,))
    vec = pl.BlockSpec((1, c), lambda i: (0, 0))
    return pl.pallas_call(
        functools.partial(_conv_step_kernel, kw=kw, nt=nt),
        grid=(b // sb,),
        in_specs=[pl.BlockSpec((te, sb, c), lambda i: (0, i, 0)),
                  pl.BlockSpec((kw, c), lambda i: (0, 0)),
                  vec, vec, vec],
        out_specs=pl.BlockSpec((nt, sb, c), lambda i: (0, i, 0)),
        out_shape=jax.ShapeDtypeStruct((nt, b, c), BF16),
        compiler_params=_params(("parallel",)),
        name="conv_step",
    )(ext_tm, conv_w, conv_b, ln_g, ln_b)


def _rwkv_vectors(pr, shifted, mu_ref, w0_ref, a0_ref, wda_ref, g2_ref, kk_ref, ka_ref, rk_ref, bd_ref,
                  outs, c, dr):
    r_o, w_o, k_o, v_o, a_o, b_o, g_o, bn_o = outs
    m = pr + (shifted - pr) * mu_ref[...]
    r = m[:, 0:c]
    k = m[:, c:2 * c]
    v = m[:, 2 * c:3 * c]
    low = m[:, 3 * c:3 * c + 2 * dr]
    lane = lax.broadcasted_iota(jnp.int32, low.shape, 1)
    low = jnp.where(lane < dr, jnp.tanh(low), low)
    dd = _dot(low.astype(BF16), wda_ref[...])
    logw = -math.exp(-0.5) * jax.nn.sigmoid(w0_ref[...] + dd[:, 0:c])
    a = jax.nn.sigmoid(a0_ref[...] + dd[:, c:2 * c])
    g = _dot(jax.nn.sigmoid(m[:, 3 * c + 2 * dr:]).astype(BF16), g2_ref[...])
    bd = bd_ref[...]
    kk = k * kk_ref[...]
    nrm = jnp.maximum(jnp.sqrt(_head_sum(kk * kk, bd)), 1e-12)
    kk = kk / nrm
    k2 = k * (1.0 + (a - 1.0) * ka_ref[...])
    r_o[...] = r
    w_o[...] = logw
    k_o[...] = k2
    v_o[...] = v
    a_o[...] = -kk
    b_o[...] = kk * a
    g_o[...] = g
    bn_o[...] = _head_sum(r * k2 * rk_ref[...], bd)


def _rwkv_prep_seq_kernel(pr_ref, prev_ref, *rest, c, dr, blocks_per_seq):
    pr = pr_ref[...]
    first = pl.program_id(0) % blocks_per_seq == 0
    prev = prev_ref[SUBLANES - 1:SUBLANES, :]
    prev = jnp.where(first, jnp.zeros_like(prev), prev)
    row = lax.broadcasted_iota(jnp.int32, pr.shape, 0)
    shifted = jnp.where(row == 0, jnp.broadcast_to(prev, pr.shape), pltpu.roll(pr, 1, 0))
    _rwkv_vectors(pr, shifted, *rest[:9], rest[9:], c, dr)


def _rwkv_prep_given_kernel(pr_ref, sh_ref, *rest, c, dr):
    _rwkv_vectors(pr_ref[...], sh_ref[...], *rest[:9], rest[9:], c, dr)


def _rwkv_prep(pr, shifted, rows, t, consts, c, dr):
    sw = pr.shape[1]
    tm = _pick(t if shifted is None else rows, (128, 64, 32, 16, 8))
    row = pl.BlockSpec((tm, sw), lambda i: (i, 0))
    out = pl.BlockSpec((tm, c), lambda i: (i, 0))
    full = lambda x: pl.BlockSpec(x.shape, lambda i: (0,) * x.ndim)
    if shifted is None:
        hb = tm // SUBLANES
        second = pl.BlockSpec((SUBLANES, sw), lambda i: (jnp.maximum(i * hb - 1, 0), 0))
        kern = functools.partial(_rwkv_prep_seq_kernel, c=c, dr=dr, blocks_per_seq=t // tm)
        second_arg = pr
    else:
        second = row
        kern = functools.partial(_rwkv_prep_given_kernel, c=c, dr=dr)
        second_arg = shifted
    return pl.pallas_call(
        kern,
        grid=(rows // tm,),
        in_specs=[row, second] + [full(x) for x in consts],
        out_specs=[out] * 8,
        out_shape=[jax.ShapeDtypeStruct((rows, c), F32)] * 8,
        compiler_params=_params(("parallel",)),
        name="rwkv_prep",
    )(pr, second_arg, *consts)


def _wkv_step_kernel(r_ref, w_ref, k_ref, v_ref, a_ref, b_ref, s0_ref, bd_ref, y_ref, st_ref,
                     *, nseq, t, g):
    bd = bd_ref[...]
    rows = lax.broadcasted_iota(jnp.int32, (HEAD, LANES), 0)
    lanes = lax.broadcasted_iota(jnp.int32, (HEAD, LANES), 1)
    diag = jnp.where(rows == lanes % HEAD, 1.0, 0.0).astype(F32)
    chains = [(q, p) for q in range(nseq) for p in range(g)]
    rr, kk, vv, aa, bb = (ref[...] for ref in (r_ref, k_ref, v_ref, a_ref, b_ref))
    ww = jnp.exp(w_ref[...])
    s = [s0_ref[q, p] for q, p in chains]
    yrows = {}
    for i in range(t):
        bc = lambda x, q, p: jnp.broadcast_to(
            x[q * t + i:q * t + i + 1, p * LANES:(p + 1) * LANES], (HEAD, LANES))
        lhs = jnp.concatenate(
            [jnp.concatenate([s[n] * bc(aa, q, p), bc(vv, q, p) * diag], axis=0)
             for n, (q, p) in enumerate(chains)], axis=0)
        res = _mm_split_lhs(lhs, bd)
        s = [s[n] * bc(ww, q, p) + res[2 * n * HEAD:(2 * n + 1) * HEAD] * bc(bb, q, p)
             + res[(2 * n + 1) * HEAD:(2 * n + 2) * HEAD] * bc(kk, q, p)
             for n, (q, p) in enumerate(chains)]
        sr = jnp.concatenate([s[n] * bc(rr, q, p) for n, (q, p) in enumerate(chains)], axis=0)
        yc = _dot(sr.astype(BF16), bd)
        for n, (q, p) in enumerate(chains):
            yrows[(q, i, p)] = jnp.sum(yc[n * HEAD:(n + 1) * HEAD] * diag, axis=0, keepdims=True)
    y_ref[...] = jnp.concatenate(
        [jnp.concatenate([yrows[(q, i, p)] for p in range(g)], axis=1)
         for q in range(nseq) for i in range(t)], axis=0)
    for n, (q, p) in enumerate(chains):
        st_ref[q, p] = s[n]


def _wkv_step(r, lw, k, v, a, b, s0, bd, t):
    rows, c = r.shape
    npair = c // LANES
    assert SUBLANES % t == 0, "token-by-token path expects a few new tokens per sequence"
    nseq = SUBLANES // t
    g = _pick(npair, (8, 4, 2, 1))
    seq = pl.BlockSpec((nseq * t, g * LANES), lambda si, pi: (si, pi))
    st = pl.BlockSpec((nseq, g, HEAD, LANES), lambda si, pi: (si, pi, 0, 0))
    return pl.pallas_call(
        functools.partial(_wkv_step_kernel, nseq=nseq, t=t, g=g),
        grid=(rows // (nseq * t), npair // g),
        in_specs=[seq] * 6 + [st, pl.BlockSpec((LANES, LANES), lambda si, pi: (0, 0))],
        out_specs=[seq, st],
        out_shape=[jax.ShapeDtypeStruct((rows, c), F32),
                   jax.ShapeDtypeStruct(s0.shape, F32)],
        compiler_params=_params(("parallel", "parallel")),
        name="wkv_step",
    )(r, lw, k, v, a, b, s0, bd)


def _chunk_masks():
    n = 2 * CHUNK
    ri = jnp.arange(n)[:, None]
    ci = jnp.arange(n)[None, :]
    levels = [(ri // 2) == (ci // 2)]
    bsz = 2
    while bsz < CHUNK:
        levels.append(((ri // (2 * bsz)) == (ci // (2 * bsz))) & ((ri // bsz) != (ci // bsz)))
        bsz *= 2
    levels.append(ri == ci)
    lv = jnp.stack(levels).astype(F32)
    r4 = jnp.arange(2 * n)[:, None]
    c4 = jnp.arange(2 * n)[None, :]
    same_head = ((r4 // CHUNK) % 2) == ((c4 // CHUNK) % 2)
    t, s = r4 % CHUNK, c4 % CHUNK
    m1 = jnp.where(r4 < n, same_head & (s < t), same_head & (s <= t)).astype(F32)
    tri = (jnp.arange(CHUNK)[:, None] >= jnp.arange(CHUNK)[None, :]).astype(BF16)
    return lv, m1, tri


def _wkv_chunk_kernel(r_ref, w_ref, k_ref, v_ref, a_ref, b_ref, s0_ref, lv_ref, m1_ref, tri_ref,
                      y_ref, st_ref, *, g):
    ci = pl.program_id(2)
    n = 2 * CHUNK
    lane = lax.broadcasted_iota(jnp.int32, (1, LANES), 1)
    m0 = jnp.where(lane < HEAD, 1.0, 0.0).astype(F32)
    m1 = 1.0 - m0
    expand = lambda x: jnp.concatenate([x * m0, x * m1], axis=0)
    fold = lambda x: x[0:CHUNK] + x[CHUNK:n]

    @pl.when(ci == 0)
    def _():
        st_ref[...] = s0_ref[...]

    pairs = range(g)
    tri = tri_ref[...]
    mask1 = m1_ref[...]
    nlev = lv_ref.shape[0]
    lv = [lv_ref[i] for i in range(nlev)]
    sls = [slice(p * LANES, (p + 1) * LANES) for p in pairs]
    r, lw, k, v, a, b = ([ref[:, sl] for sl in sls] for ref in (r_ref, w_ref, k_ref, v_ref, a_ref, b_ref))
    s_old = [expand(st_ref[0, p]) for p in pairs]
    lw_split = [_split(x) for x in lw]
    cum = [_dot(tri, hi) + _dot(tri, lo) for hi, lo in lw_split]
    cum_l = [x[CHUNK - 1:CHUNK] for x in cum]
    e_neg = [jnp.exp(-x) for x in cum]
    e_hat = [jnp.exp(cl - x) for cl, x in zip(cum_l, cum)]
    at_e = [expand(a[p] * jnp.exp(cum[p] - lw[p])) for p in pairs]
    rt_e = [expand(r[p] * jnp.exp(cum[p])) for p in pairs]
    v_e = [expand(x) for x in v]
    bt = [(b[p] * e_neg[p]).astype(BF16) for p in pairs]
    kt = [(k[p] * e_neg[p]).astype(BF16) for p in pairs]
    nt = (((1,), (1,)), ((), ()))
    out1 = [lax.dot_general(jnp.concatenate([at_e[p], rt_e[p]], axis=0).astype(BF16),
                            jnp.concatenate([bt[p], bt[p], kt[p], kt[p]], axis=0), nt,
                            preferred_element_type=F32) for p in pairs]
    out1 = [jnp.where(mask1 != 0.0, x, 0.0) for x in out1]
    a_ab = [x[0:n, 0:n] for x in out1]
    m_r = [x[n:2 * n, :].astype(BF16) for x in out1]
    akv = [_dot(out1[p][0:n, n:2 * n].astype(BF16), v_e[p].astype(BF16)) for p in pairs]
    tm = [lv[nlev - 1] + x * lv[0] for x in a_ab]
    for lev in range(1, nlev - 1):
        step = [_mm_split_lhs(tm[p], a_ab[p] * lv[lev]) for p in pairs]
        tm = [tm[p] + _mm_split_lhs(step[p], tm[p]) for p in pairs]
    wu = [_mm_split_both(tm[p], jnp.concatenate([at_e[p], akv[p]], axis=1)) for p in pairs]
    rhs4 = [jnp.concatenate([wu[p], jnp.concatenate([jnp.zeros_like(v_e[p]), v_e[p]], axis=1)],
                            axis=0).astype(BF16) for p in pairs]
    o4 = [_dot(m_r[p], rhs4[p]) for p in pairs]
    q = [fold(rt_e[p] + o4[p][:, 0:LANES]).astype(BF16) for p in pairs]
    y1 = [fold(o4[p][:, LANES:2 * LANES]) for p in pairs]
    rhs5 = [jnp.concatenate([expand(b[p] * e_hat[p]), expand(k[p] * e_hat[p])], axis=0).astype(BF16)
            for p in pairs]
    o5 = [lax.dot_general(rhs4[p], rhs5[p], (((0,), (0,)), ((), ())), preferred_element_type=F32)
          for p in pairs]
    s_bf = [x.astype(BF16) for x in s_old]
    y = [lax.dot_general(q[p], s_bf[p], nt, preferred_element_type=F32) + y1[p] for p in pairs]
    s_new = [s_old[p] * jnp.exp(cum_l[p]) + _dot(s_bf[p], o5[p][0:n].astype(BF16)) + o5[p][n:2 * n]
             for p in pairs]
    for p in pairs:
        y_ref[:, sls[p]] = y[p]
        st_ref[0, p] = fold(s_new[p])


def _wkv_chunk(r, lw, k, v, a, b, s0, t):
    rows, c = r.shape
    s = rows // t
    npair = c // LANES
    nc = t // CHUNK
    g = _pick(npair, (8, 4, 2, 1))
    lv, m1, tri = _chunk_masks()
    seq = pl.BlockSpec((CHUNK, g * LANES), lambda si, pi, ci: (si * nc + ci, pi))
    st = pl.BlockSpec((1, g, HEAD, LANES), lambda si, pi, ci: (si, pi, 0, 0))
    const = lambda x: pl.BlockSpec(x.shape, lambda si, pi, ci: (0,) * x.ndim)
    return pl.pallas_call(
        functools.partial(_wkv_chunk_kernel, g=g),
        grid=(s, npair // g, nc),
        in_specs=[seq] * 6 + [st, const(lv), const(m1), const(tri)],
        out_specs=[seq, st],
        out_shape=[jax.ShapeDtypeStruct((rows, c), F32),
                   jax.ShapeDtypeStruct((s, npair, HEAD, LANES), F32)],
        compiler_params=_params(("parallel", "parallel", "arbitrary")),
        name="wkv_chunk",
    )(r, lw, k, v, a, b, s0, lv, m1, tri)


def _rwkv_post_kernel(y_ref, bn_ref, v_ref, g_ref, gg_ref, gb_ref, bd_ref, o_ref):
    y = y_ref[...]
    bd = bd_ref[...]
    ym = _head_sum(y, bd) * (1.0 / HEAD)
    yc = y - ym
    yv = _head_sum(yc * yc, bd) * (1.0 / HEAD)
    yn = yc * lax.rsqrt(yv + GN_EPS) * gg_ref[...] + gb_ref[...]
    o_ref[...] = ((yn + bn_ref[...] * v_ref[...]) * g_ref[...]).astype(o_ref.dtype)


def _rwkv_post(y, bn, v, g, gn_g, gn_b, bd):
    m, c = y.shape
    tm = _pick(m, (256, 128, 64, 32, 16, 8))
    row = pl.BlockSpec((tm, c), lambda i: (i, 0))
    vec = pl.BlockSpec((1, c), lambda i: (0, 0))
    return pl.pallas_call(
        _rwkv_post_kernel,
        grid=(m // tm,),
        in_specs=[row, row, row, row, vec, vec, pl.BlockSpec((LANES, LANES), lambda i: (0, 0))],
        out_specs=row,
        out_shape=jax.ShapeDtypeStruct((m, c), BF16),
        compiler_params=_params(("parallel",)),
        name="rwkv_post",
    )(y, bn, v, g, gn_g, gn_b, bd)


def _pack_state(s):
    n, h = s.shape[0], s.shape[1]
    return s.reshape(n, h // 2, 2, HEAD, HEAD).transpose(0, 1, 3, 2, 4).reshape(n, h // 2, HEAD, 2 * HEAD)


def _unpack_state(s):
    n, hp = s.shape[0], s.shape[1]
    return s.reshape(n, hp, HEAD, 2, HEAD).transpose(0, 1, 3, 2, 4).reshape(n, 2 * hp, HEAD, HEAD)


def _layer(x_prompt, x_sample, wkv0, conv0, shift0,
           ln_mix_pre, ln_mix_post, ln_ffn_pre, ln_ffn_post, w_in, b_gate,
           conv_w, conv_b, conv_ln_g, conv_ln_b, w_conv_out, shift_mu,
           w0, w2, a0, a2, g2, k_k, k_a, r_k, gn_g, gn_b, w_rwkv_out, w_o,
           w_ffn_gate, w_ffn_up, w_ffn_down):
    bp, tp, d = x_prompt.shape
    bs, ts, _ = x_sample.shape
    c = conv_w.shape[1]
    kw = conv_w.shape[0]
    dr = w2.shape[0]
    sw = shift_mu.shape[0]
    mp, ms = bp * tp, bs * ts
    assert tp % CHUNK == 0 and tp >= kw - 1, "prompt sequences are processed in 64-token chunks"
    tm = _pick(math.gcd(mp, ms), (512, 256, 128, 64, 32, 16))
    row = lambda x: x.reshape(1, -1)

    w_glu_a = w_in[:, 0:c].astype(BF16)
    w_glu_b = w_in[:, c:2 * c].astype(BF16)
    w_shift = w_in[:, 2 * c:2 * c + sw].astype(BF16)
    w_g0 = w_in[:, 2 * c + sw:2 * c + sw + d].astype(BF16)
    w_g1 = w_in[:, 2 * c + sw + d:].astype(BF16)
    zeros = jnp.zeros((dr, c), F32)
    wda = jnp.concatenate([jnp.concatenate([w2, zeros], axis=1),
                           jnp.concatenate([zeros, a2], axis=1)], axis=0).astype(BF16)
    idx = jnp.arange(LANES) // HEAD
    bd = (idx[:, None] == idx[None, :]).astype(BF16)

    x_p = x_prompt.reshape(mp, d)
    x_s = x_sample.reshape(ms, d)
    h = _rms_cast(x_p, x_s, row(ln_mix_pre), min(tm, 256))

    conv_args = (conv_w, row(conv_b), row(conv_ln_g), row(conv_ln_b))
    u = _mm_glu(h, w_glu_a, w_glu_b)
    u_s = u[mp:].reshape(bs, ts, c)
    zc_p = _conv_seq(u, bp, tp, *conv_args)
    ext_s = jnp.concatenate([conv0, u_s], axis=1)
    zc_s = _conv_step(ext_s.transpose(1, 0, 2), *conv_args).transpose(1, 0, 2).reshape(ms, c)
    conv_p = u[:mp].reshape(bp, tp, c)[:, tp - (kw - 1):]
    conv_s = ext_s[:, ts:]

    pr = _mm(h, w_shift, name="mm_shift")
    pr_s = pr[mp:].reshape(bs, ts, sw)
    shifted_s = jnp.concatenate([shift0[:, None], pr_s[:, :-1]], axis=1).reshape(ms, sw)
    consts = (row(shift_mu), row(w0), row(a0), wda, g2.astype(BF16), row(k_k), row(k_a), row(r_k), bd)
    vec_p = _rwkv_prep(pr, None, mp, tp, consts, c, dr)
    vec_s = _rwkv_prep(pr_s.reshape(ms, sw), shifted_s, ms, ts, consts, c, dr)
    y_p, st_p = _wkv_chunk(*vec_p[:6], jnp.zeros((bp, c // LANES, HEAD, LANES), F32), tp)
    y_s, st_s = _wkv_step(*vec_s[:6], _pack_state(wkv0), bd, ts)
    yr_p = _rwkv_post(y_p, vec_p[7], vec_p[3], vec_p[6], row(gn_g), row(gn_b), bd)
    yr_s = _rwkv_post(y_s, vec_s[7], vec_s[3], vec_s[6], row(gn_g), row(gn_b), bd)
    shift_p = pr[:mp].reshape(bp, tp, sw)[:, -1]
    shift_s = pr_s[:, -1]

    mixed = _merge(h, zc_p, zc_s, yr_p, yr_s, w_g0, w_g1, w_conv_out.astype(BF16),
                   w_rwkv_out.astype(BF16), b_gate[0:1], b_gate[1:2], tm)
    o = _mm(mixed, w_o.astype(BF16), name="mm_o")
    x1, h2 = _rms_res(x_p, x_s, o, row(ln_mix_post), row(ln_ffn_pre), min(tm, 256))

    act = _mm_swiglu(h2, w_ffn_gate.astype(BF16), w_ffn_up.astype(BF16))
    f = _mm(act, w_ffn_down.astype(BF16), name="mm_down")
    tl = min(tm, 256)
    y_prompt = _rms_res_last(x1, f, row(ln_ffn_post), 0, mp, tl)
    y_sample = _rms_res_last(x1, f, row(ln_ffn_post), mp, ms, tl)

    return (y_prompt.reshape(bp, tp, d), y_sample.reshape(bs, ts, d),
            _unpack_state(st_p), conv_p, shift_p,
            _unpack_state(st_s), conv_s, shift_s)


def kernel(x_prompt, x_sample, state_wkv, state_conv, state_shift, ln_mix_pre, ln_mix_post, ln_ffn_pre,
           ln_ffn_post, w_in, b_gate, conv_w, conv_b, conv_ln_g, conv_ln_b, w_conv_out, shift_mu, w0, w2,
           a0, a2, g2, k_k, k_a, r_k, gn_g, gn_b, w_rwkv_out, w_o, w_ffn_gate, w_ffn_up, w_ffn_down):
    depth = w_in.shape[0]
    assert depth == 1, "one decoder layer per step"
    weights = (ln_mix_pre, ln_mix_post, ln_ffn_pre, ln_ffn_post, w_in, b_gate,
               conv_w, conv_b, conv_ln_g, conv_ln_b, w_conv_out, shift_mu,
               w0, w2, a0, a2, g2, k_k, k_a, r_k, gn_g, gn_b, w_rwkv_out, w_o,
               w_ffn_gate, w_ffn_up, w_ffn_down)
    lw = tuple(wt[0] for wt in weights)
    yp, ys, wkv_p, conv_p, shift_p, wkv_s, conv_s, shift_s = _layer(
        x_prompt, x_sample, state_wkv[0], state_conv[0], state_shift[0], *lw)
    return (yp, ys, wkv_p[None], conv_p[None], shift_p[None],
            wkv_s[None], conv_s[None], shift_s[None])
```

```python
import functools
import math

import jax
import jax.numpy as jnp
from jax import lax
from jax.experimental import pallas as pl
from jax.experimental.pallas import tpu as pltpu

F32 = jnp.float32
BF16 = jnp.bfloat16

RMS_EPS = 1e-6
LN_EPS = 1e-5
GN_EPS = 64e-5
HEAD = 64
LANES = 128
SUBLANES = 8
CONV_HALO = 32
CHUNK = 64
VMEM_LIMIT = 56 * 1024 * 1024


def _pick(n, cands):
    for c in cands:
        if n % c == 0:
            return c
    return n


def _params(sem):
    return pltpu.CompilerParams(dimension_semantics=sem, vmem_limit_bytes=VMEM_LIMIT)


def _dot(a, b):
    return jnp.dot(a, b, preferred_element_type=F32)


def _split(x):
    hi = x.astype(BF16)
    return hi, (x - hi.astype(F32)).astype(BF16)


def _mm_split_lhs(a, b):
    hi, lo = _split(a)
    bb = b.astype(BF16)
    return _dot(hi, bb) + _dot(lo, bb)


def _mm_split_both(a, b):
    ah, al = _split(a)
    bh, bl = _split(b)
    return _dot(ah, bh) + _dot(al, bh) + _dot(ah, bl)


def _head_sum(x, bd):
    cols = [_dot(x[:, c:c + LANES].astype(BF16), bd) for c in range(0, x.shape[1], LANES)]
    return jnp.concatenate(cols, axis=1)


def _two_group_specs(tm, d, nblk_p):
    return (pl.BlockSpec((tm, d), lambda i, *_: (jnp.minimum(i, nblk_p - 1), 0)),
            pl.BlockSpec((tm, d), lambda i, *_: (jnp.maximum(i - nblk_p, 0), 0)))


def _two_group_rows(p_ref, s_ref, nblk_p):
    return jnp.where(pl.program_id(0) < nblk_p, p_ref[...], s_ref[...])


def _rms_cast_kernel(xp_ref, xs_ref, g_ref, o_ref, *, nblk_p):
    x = _two_group_rows(xp_ref, xs_ref, nblk_p)
    ms = jnp.mean(x * x, axis=-1, keepdims=True)
    o_ref[...] = (x * lax.rsqrt(ms + RMS_EPS) * g_ref[...]).astype(o_ref.dtype)


def _rms_cast(x_p, x_s, g, tm):
    (mp, d), ms = x_p.shape, x_s.shape[0]
    xp_spec, xs_spec = _two_group_specs(tm, d, mp // tm)
    return pl.pallas_call(
        functools.partial(_rms_cast_kernel, nblk_p=mp // tm),
        grid=((mp + ms) // tm,),
        in_specs=[xp_spec, xs_spec, pl.BlockSpec((1, d), lambda i: (0, 0))],
        out_specs=pl.BlockSpec((tm, d), lambda i: (i, 0)),
        out_shape=jax.ShapeDtypeStruct((mp + ms, d), BF16),
        compiler_params=_params(("parallel",)),
        name="rms_cast",
    )(x_p, x_s, g)


def _rms_res_kernel(xp_ref, xs_ref, o_ref, g_ref, g2_ref, x1_ref, h_ref, *, nblk_p):
    o = o_ref[...]
    ms = jnp.mean(o * o, axis=-1, keepdims=True)
    x1 = _two_group_rows(xp_ref, xs_ref, nblk_p) + o * lax.rsqrt(ms + RMS_EPS) * g_ref[...]
    x1_ref[...] = x1
    ms1 = jnp.mean(x1 * x1, axis=-1, keepdims=True)
    h_ref[...] = (x1 * lax.rsqrt(ms1 + RMS_EPS) * g2_ref[...]).astype(h_ref.dtype)


def _rms_res(x_p, x_s, o, g, g2, tm):
    (mp, d), m = x_p.shape, o.shape[0]
    xp_spec, xs_spec = _two_group_specs(tm, d, mp // tm)
    row = pl.BlockSpec((tm, d), lambda i: (i, 0))
    vec = pl.BlockSpec((1, d), lambda i: (0, 0))
    return pl.pallas_call(
        functools.partial(_rms_res_kernel, nblk_p=mp // tm),
        grid=(m // tm,),
        in_specs=[xp_spec, xs_spec, row, vec, vec],
        out_specs=[row, row],
        out_shape=[jax.ShapeDtypeStruct((m, d), F32), jax.ShapeDtypeStruct((m, d), BF16)],
        compiler_params=_params(("parallel",)),
        name="rms_res",
    )(x_p, x_s, o, g, g2)


def _rms_res_last_kernel(x_ref, o_ref, g_ref, x1_ref):
    o = o_ref[...]
    ms = jnp.mean(o * o, axis=-1, keepdims=True)
    x1_ref[...] = x_ref[...] + o * lax.rsqrt(ms + RMS_EPS) * g_ref[...]


def _rms_res_last(x, o, g, row0, rows, tm):
    d = x.shape[1]
    blk0 = row0 // tm
    src = pl.BlockSpec((tm, d), lambda i: (blk0 + i, 0))
    return pl.pallas_call(
        _rms_res_last_kernel,
        grid=(rows // tm,),
        in_specs=[src, src, pl.BlockSpec((1, d), lambda i: (0, 0))],
        out_specs=pl.BlockSpec((tm, d), lambda i: (i, 0)),
        out_shape=jax.ShapeDtypeStruct((rows, d), F32),
        compiler_params=_params(("parallel",)),
        name="rms_res_last",
    )(x, o, g)


MM_VMEM_BUDGET = 44 * 1024 * 1024


def _mm_tiles(m, k, n, col0s, w_bytes, n_w, out_bytes):
    tn = _pick(math.gcd(n, *col0s), (256, 128))
    for tm in (1088, 1024, 544, 512, 256, 128, 64, 32, 16, 8):
        need = (2 * tm * k * 2
                + n_w * k * tn * (2 * w_bytes + 2)
                + tm * tn * (2 * out_bytes + 4 * n_w))
        if m % tm == 0 and need <= MM_VMEM_BUDGET:
            return tm, tn
    return m, tn


def _mm_kernel(x_ref, w_ref, o_ref):
    o_ref[...] = _dot(x_ref[...], w_ref[...].astype(BF16)).astype(o_ref.dtype)


def _mm(x, w, n=None, col0=0, out_dtype=F32, name="mm"):
    m, k = x.shape
    n = w.shape[1] if n is None else n
    tm, tn = _mm_tiles(m, k, n, (col0,), w.dtype.itemsize, 1, jnp.dtype(out_dtype).itemsize)
    j0 = col0 // tn
    return pl.pallas_call(
        _mm_kernel,
        grid=(m // tm, n // tn),
        in_specs=[pl.BlockSpec((tm, k), lambda i, j: (i, 0)),
                  pl.BlockSpec((k, tn), lambda i, j: (0, j0 + j))],
        out_specs=pl.BlockSpec((tm, tn), lambda i, j: (i, j)),
        out_shape=jax.ShapeDtypeStruct((m, n), out_dtype),
        compiler_params=_params(("parallel", "arbitrary")),
        name=name,
    )(x, w)


def _mm_glu_kernel(x_ref, wa_ref, wb_ref, o_ref):
    x = x_ref[...]
    o_ref[...] = _dot(x, wa_ref[...].astype(BF16)) * jax.nn.sigmoid(_dot(x, wb_ref[...].astype(BF16)))


def _mm_swiglu_kernel(x_ref, wg_ref, wu_ref, o_ref):
    x = x_ref[...]
    o_ref[...] = (jax.nn.silu(_dot(x, wg_ref[...].astype(BF16)))
                  * _dot(x, wu_ref[...].astype(BF16))).astype(o_ref.dtype)


def _mm_pair(kern, x, wa, col_a, wb, col_b, n, out_dtype, name):
    m, k = x.shape
    tm, tn = _mm_tiles(m, k, n, (col_a, col_b), wa.dtype.itemsize, 2, jnp.dtype(out_dtype).itemsize)
    ja, jb = col_a // tn, col_b // tn
    return pl.pallas_call(
        kern,
        grid=(m // tm, n // tn),
        in_specs=[pl.BlockSpec((tm, k), lambda i, j: (i, 0)),
                  pl.BlockSpec((k, tn), lambda i, j: (0, ja + j)),
                  pl.BlockSpec((k, tn), lambda i, j: (0, jb + j))],
        out_specs=pl.BlockSpec((tm, tn), lambda i, j: (i, j)),
        out_shape=jax.ShapeDtypeStruct((m, n), out_dtype),
        compiler_params=_params(("parallel", "arbitrary")),
        name=name,
    )(x, wa, wb)


def _merge_kernel(h_ref, zcp_ref, zcs_ref, yrp_ref, yrs_ref, wg0_ref, wg1_ref, wc_ref, wr_ref,
                  b0_ref, b1_ref, o_ref, *, nblk_p):
    h = h_ref[...]
    g0 = jax.nn.sigmoid(_dot(h, wg0_ref[...]) + b0_ref[...])
    g1 = jax.nn.sigmoid(_dot(h, wg1_ref[...]) + b1_ref[...])
    oc = _dot(_two_group_rows(zcp_ref, zcs_ref, nblk_p), wc_ref[...])
    orr = _dot(_two_group_rows(yrp_ref, yrs_ref, nblk_p), wr_ref[...])
    o_ref[...] = (g0 * oc + g1 * orr).astype(o_ref.dtype)


def _merge(h, zc_p, zc_s, yr_p, yr_s, wg0, wg1, wc, wr, b0, b1, tm):
    m, d = h.shape
    mp, c = zc_p.shape
    tn = _pick(d, (256, 128))
    gp_spec, gs_spec = _two_group_specs(tm, c, mp // tm)
    return pl.pallas_call(
        functools.partial(_merge_kernel, nblk_p=mp // tm),
        grid=(m // tm, d // tn),
        in_specs=[pl.BlockSpec((tm, d), lambda i, j: (i, 0)),
                  gp_spec, gs_spec, gp_spec, gs_spec,
                  pl.BlockSpec((d, tn), lambda i, j: (0, j)),
                  pl.BlockSpec((d, tn), lambda i, j: (0, j)),
                  pl.BlockSpec((c, tn), lambda i, j: (0, j)),
                  pl.BlockSpec((c, tn), lambda i, j: (0, j)),
                  pl.BlockSpec((1, tn), lambda i, j: (0, j)),
                  pl.BlockSpec((1, tn), lambda i, j: (0, j))],
        out_specs=pl.BlockSpec((tm, tn), lambda i, j: (i, j)),
        out_shape=jax.ShapeDtypeStruct((m, d), BF16),
        compiler_params=_params(("parallel", "arbitrary")),
        name="merge",
    )(h, zc_p, zc_s, yr_p, yr_s, wg0, wg1, wc, wr, b0, b1)


def _ln_silu(z, g, b):
    mu = jnp.mean(z, axis=-1, keepdims=True)
    zc = z - mu
    var = jnp.mean(zc * zc, axis=-1, keepdims=True)
    y = zc * lax.rsqrt(var + LN_EPS) * g + b
    return y * jax.nn.sigmoid(y)


def _conv_seq_kernel(um_ref, uh_ref, w_ref, cb_ref, g_ref, b_ref, o_ref, x_scr, xs_scr, z_scr,
                     *, tb, kw, lc):
    i = pl.program_id(1)
    c = um_ref.shape[1]
    halo = uh_ref[...]
    x_scr[0:CONV_HALO, :] = jnp.where(i == 0, jnp.zeros_like(halo), halo)
    x_scr[CONV_HALO:CONV_HALO + tb, :] = um_ref[...]
    off = CONV_HALO - (kw - 1)
    first = True
    for s in range(SUBLANES):
        taps = [j for j in range(kw) if (off + j) % SUBLANES == s]
        if not taps:
            continue
        span = max(off + j - s for j in taps) + tb
        xs_scr[0:span, :] = x_scr[s:s + span, :]
        for c0 in range(0, c, lc):
            cols = slice(c0, c0 + lc)
            acc = cb_ref[:, cols] if first else z_scr[:, cols]
            for j in taps:
                a0 = off + j - s
                acc = acc + xs_scr[a0:a0 + tb, cols] * w_ref[j:j + 1, cols]
            z_scr[:, cols] = acc
        first = False
    o_ref[...] = _ln_silu(z_scr[...], g_ref[...], b_ref[...]).astype(o_ref.dtype)


def _conv_seq(u, nseq, t, conv_w, conv_b, ln_g, ln_b):
    c = u.shape[1]
    kw = conv_w.shape[0]
    tb = _pick(t, (64, 32))
    lc = _pick(c, (256, 128))
    nb = t // tb
    hb = tb // CONV_HALO
    vec = pl.BlockSpec((1, c), lambda bi, i: (0, 0))
    return pl.pallas_call(
        functools.partial(_conv_seq_kernel, tb=tb, kw=kw, lc=lc),
        grid=(nseq, nb),
        in_specs=[pl.BlockSpec((tb, c), lambda bi, i: (bi * nb + i, 0)),
                  pl.BlockSpec((CONV_HALO, c), lambda bi, i: (jnp.maximum((bi * nb + i) * hb - 1, 0), 0)),
                  pl.BlockSpec((kw, c), lambda bi, i: (0, 0)),
                  vec, vec, vec],
        out_specs=pl.BlockSpec((tb, c), lambda bi, i: (bi * nb + i, 0)),
        out_shape=jax.ShapeDtypeStruct((nseq * t, c), BF16),
        scratch_shapes=[pltpu.VMEM((CONV_HALO + tb, c), F32), pltpu.VMEM((CONV_HALO + tb, c), F32),
                        pltpu.VMEM((tb, c), F32)],
        compiler_params=_params(("parallel", "arbitrary")),
        name="conv_seq",
    )(u, u, conv_w, conv_b, ln_g, ln_b)


def _conv_step_kernel(e_ref, w_ref, cb_ref, g_ref, b_ref, o_ref, *, kw, nt):
    for t in range(nt):
        acc = e_ref[t] * w_ref[0:1, :]
        for j in range(1, kw):
            acc = acc + e_ref[t + j] * w_ref[j:j + 1, :]
        o_ref[t] = _ln_silu(acc + cb_ref[...], g_ref[...], b_ref[...]).astype(o_ref.dtype)


def _conv_step(ext_tm, conv_w, conv_b, ln_g, ln_b):
    te, b, c = ext_tm.shape
    kw = conv_w.shape[0]
    nt = te - (kw - 1)
    sb = _pick(b, (8,))
    vec = pl.BlockSpec((1, c), lambda i: (0, 0))
    return pl.pallas_call(
        functools.partial(_conv_step_kernel, kw=kw, nt=nt),
        grid=(b // sb,),
        in_specs=[pl.BlockSpec((te, sb, c), lambda i: (0, i, 0)),
                  pl.BlockSpec((kw, c), lambda i: (0, 0)),
                  vec, vec, vec],
        out_specs=pl.BlockSpec((nt, sb, c), lambda i: (0, i, 0)),
        out_shape=jax.ShapeDtypeStruct((nt, b, c), BF16),
        compiler_params=_params(("parallel",)),
        name="conv_step",
    )(ext_tm, conv_w, conv_b, ln_g, ln_b)


def _rwkv_vectors(pr, shifted, mu_ref, w0_ref, a0_ref, wda_ref, g2_ref, kk_ref, ka_ref, rk_ref, bd_ref,
                  outs, c, dr):
    r_o, w_o, k_o, v_o, a_o, b_o, g_o, bn_o = outs
    m = pr + (shifted - pr) * mu_ref[...]
    r = m[:, 0:c]
    k = m[:, c:2 * c]
    v = m[:, 2 * c:3 * c]
    low = m[:, 3 * c:3 * c + 2 * dr]
    lane = lax.broadcasted_iota(jnp.int32, low.shape, 1)
    low = jnp.where(lane < dr, jnp.tanh(low), low)
    dd = _dot(low.astype(BF16), wda_ref[...])
    logw = -math.exp(-0.5) * jax.nn.sigmoid(w0_ref[...] + dd[:, 0:c])
    a = jax.nn.sigmoid(a0_ref[...] + dd[:, c:2 * c])
    g = _dot(jax.nn.sigmoid(m[:, 3 * c + 2 * dr:]).astype(BF16), g2_ref[...])
    bd = bd_ref[...]
    kk = k * kk_ref[...]
    nrm = jnp.maximum(jnp.sqrt(_head_sum(kk * kk, bd)), 1e-12)
    kk = kk / nrm
    k2 = k * (1.0 + (a - 1.0) * ka_ref[...])
    r_o[...] = r
    w_o[...] = logw
    k_o[...] = k2
    v_o[...] = v
    a_o[...] = -kk
    b_o[...] = kk * a
    g_o[...] = g
    bn_o[...] = _head_sum(r * k2 * rk_ref[...], bd)


def _rwkv_prep_seq_kernel(pr_ref, prev_ref, *rest, c, dr, blocks_per_seq):
    pr = pr_ref[...]
    first = pl.program_id(0) % blocks_per_seq == 0
    prev = prev_ref[SUBLANES - 1:SUBLANES, :]
    prev = jnp.where(first, jnp.zeros_like(prev), prev)
    row = lax.broadcasted_iota(jnp.int32, pr.shape, 0)
    shifted = jnp.where(row == 0, jnp.broadcast_to(prev, pr.shape), pltpu.roll(pr, 1, 0))
    _rwkv_vectors(pr, shifted, *rest[:9], rest[9:], c, dr)


def _rwkv_prep_given_kernel(pr_ref, sh_ref, *rest, c, dr):
    _rwkv_vectors(pr_ref[...], sh_ref[...], *rest[:9], rest[9:], c, dr)


def _rwkv_prep(pr, shifted, rows, t, consts, c, dr):
    sw = pr.shape[1]
    tm = _pick(t if shifted is None else rows, (128, 64, 32, 16, 8))
    row = pl.BlockSpec((tm, sw), lambda i: (i, 0))
    out = pl.BlockSpec((tm, c), lambda i: (i, 0))
    full = lambda x: pl.BlockSpec(x.shape, lambda i: (0,) * x.ndim)
    if shifted is None:
        hb = tm // SUBLANES
        second = pl.BlockSpec((SUBLANES, sw), lambda i: (jnp.maximum(i * hb - 1, 0), 0))
        kern = functools.partial(_rwkv_prep_seq_kernel, c=c, dr=dr, blocks_per_seq=t // tm)
        second_arg = pr
    else:
        second = row
        kern = functools.partial(_rwkv_prep_given_kernel, c=c, dr=dr)
        second_arg = shifted
    return pl.pallas_call(
        kern,
        grid=(rows // tm,),
        in_specs=[row, second] + [full(x) for x in consts],
        out_specs=[out] * 8,
        out_shape=[jax.ShapeDtypeStruct((rows, c), F32)] * 8,
        compiler_params=_params(("parallel",)),
        name="rwkv_prep",
    )(pr, second_arg, *consts)


def _wkv_step_kernel(r_ref, w_ref, k_ref, v_ref, a_ref, b_ref, s0_ref, bd_ref, y_ref, st_ref,
                     *, nseq, t, g):
    bd = bd_ref[...]
    rows = lax.broadcasted_iota(jnp.int32, (HEAD, LANES), 0)
    lanes = lax.broadcasted_iota(jnp.int32, (HEAD, LANES), 1)
    diag = jnp.where(rows == lanes % HEAD, 1.0, 0.0).astype(F32)
    chains = [(q, p) for q in range(nseq) for p in range(g)]
    rr, kk, vv, aa, bb = (ref[...] for ref in (r_ref, k_ref, v_ref, a_ref, b_ref))
    ww = jnp.exp(w_ref[...])
    s = [s0_ref[q, p] for q, p in chains]
    yrows = {}
    for i in range(t):
        bc = lambda x, q, p: jnp.broadcast_to(
            x[q * t + i:q * t + i + 1, p * LANES:(p + 1) * LANES], (HEAD, LANES))
        lhs = jnp.concatenate(
            [jnp.concatenate([s[n] * bc(aa, q, p), bc(vv, q, p) * diag], axis=0)
             for n, (q, p) in enumerate(chains)], axis=0)
        res = _dot(lhs.astype(BF16), bd)
        s = [s[n] * bc(ww, q, p) + res[2 * n * HEAD:(2 * n + 1) * HEAD] * bc(bb, q, p)
             + res[(2 * n + 1) * HEAD:(2 * n + 2) * HEAD] * bc(kk, q, p)
             for n, (q, p) in enumerate(chains)]
        sr = jnp.concatenate([s[n] * bc(rr, q, p) for n, (q, p) in enumerate(chains)], axis=0)
        yc = _dot(sr.astype(BF16), bd)
        for n, (q, p) in enumerate(chains):
            yrows[(q, i, p)] = jnp.sum(yc[n * HEAD:(n + 1) * HEAD] * diag, axis=0, keepdims=True)
    y_ref[...] = jnp.concatenate(
        [jnp.concatenate([yrows[(q, i, p)] for p in range(g)], axis=1)
         for q in range(nseq) for i in range(t)], axis=0)
    for n, (q, p) in enumerate(chains):
        st_ref[q, p] = s[n]


def _wkv_step(r, lw, k, v, a, b, s0, bd, t):
    rows, c = r.shape
    npair = c // LANES
    assert SUBLANES % t == 0, "token-by-token path expects a few new tokens per sequence"
    nseq = SUBLANES // t
    g = _pick(npair, (8, 4, 2, 1))
    seq = pl.BlockSpec((nseq * t, g * LANES), lambda si, pi: (si, pi))
    st = pl.BlockSpec((nseq, g, HEAD, LANES), lambda si, pi: (si, pi, 0, 0))
    return pl.pallas_call(
        functools.partial(_wkv_step_kernel, nseq=nseq, t=t, g=g),
        grid=(rows // (nseq * t), npair // g),
        in_specs=[seq] * 6 + [st, pl.BlockSpec((LANES, LANES), lambda si, pi: (0, 0))],
        out_specs=[seq, st],
        out_shape=[jax.ShapeDtypeStruct((rows, c), F32),
                   jax.ShapeDtypeStruct(s0.shape, F32)],
        compiler_params=_params(("parallel", "parallel")),
        name="wkv_step",
    )(r, lw, k, v, a, b, s0, bd)


def _chunk_masks():
    n = 2 * CHUNK
    ri = jnp.arange(n)[:, None]
    ci = jnp.arange(n)[None, :]
    levels = [(ri // 2) == (ci // 2)]
    bsz = 2
    while bsz < CHUNK:
        levels.append(((ri // (2 * bsz)) == (ci // (2 * bsz))) & ((ri // bsz) != (ci // bsz)))
        bsz *= 2
    levels.append(ri == ci)
    lv = jnp.stack(levels).astype(F32)
    r4 = jnp.arange(2 * n)[:, None]
    c4 = jnp.arange(2 * n)[None, :]
    same_head = ((r4 // CHUNK) % 2) == ((c4 // CHUNK) % 2)
    t, s = r4 % CHUNK, c4 % CHUNK
    m1 = jnp.where(r4 < n, same_head & (s < t), same_head & (s <= t)).astype(F32)
    tri = (jnp.arange(CHUNK)[:, None] >= jnp.arange(CHUNK)[None, :]).astype(BF16)
    return lv, m1, tri


def _wkv_chunk_kernel(r_ref, w_ref, k_ref, v_ref, a_ref, b_ref, s0_ref, lv_ref, m1_ref, tri_ref,
                      y_ref, st_ref, *, g):
    ci = pl.program_id(2)
    n = 2 * CHUNK
    lane = lax.broadcasted_iota(jnp.int32, (1, LANES), 1)
    m0 = jnp.where(lane < HEAD, 1.0, 0.0).astype(F32)
    m1 = 1.0 - m0
    expand = lambda x: jnp.concatenate([x * m0, x * m1], axis=0)
    fold = lambda x: x[0:CHUNK] + x[CHUNK:n]

    @pl.when(ci == 0)
    def _():
        st_ref[...] = s0_ref[...]

    pairs = range(g)
    tri = tri_ref[...]
    mask1 = m1_ref[...]
    nlev = lv_ref.shape[0]
    lv = [lv_ref[i] for i in range(nlev)]
    sls = [slice(p * LANES, (p + 1) * LANES) for p in pairs]
    r, lw, k, v, a, b = ([ref[:, sl] for sl in sls] for ref in (r_ref, w_ref, k_ref, v_ref, a_ref, b_ref))
    s_old = [expand(st_ref[0, p]) for p in pairs]
    lw_split = [_split(x) for x in lw]
    cum = [_dot(tri, hi) + _dot(tri, lo) for hi, lo in lw_split]
    cum_l = [x[CHUNK - 1:CHUNK] for x in cum]
    e_neg = [jnp.exp(-x) for x in cum]
    e_hat = [jnp.exp(cl - x) for cl, x in zip(cum_l, cum)]
    at_e = [expand(a[p] * jnp.exp(cum[p] - lw[p])) for p in pairs]
    rt_e = [expand(r[p] * jnp.exp(cum[p])) for p in pairs]
    v_e = [expand(x) for x in v]
    bt = [(b[p] * e_neg[p]).astype(BF16) for p in pairs]
    kt = [(k[p] * e_neg[p]).astype(BF16) for p in pairs]
    nt = (((1,), (1,)), ((), ()))
    out1 = [lax.dot_general(jnp.concatenate([at_e[p], rt_e[p]], axis=0).astype(BF16),
                            jnp.concatenate([bt[p], bt[p], kt[p], kt[p]], axis=0), nt,
                            preferred_element_type=F32) for p in pairs]
    out1 = [jnp.where(mask1 != 0.0, x, 0.0) for x in out1]
    a_ab = [x[0:n, 0:n] for x in out1]
    m_r = [x[n:2 * n, :].astype(BF16) for x in out1]
    akv = [_dot(out1[p][0:n, n:2 * n].astype(BF16), v_e[p].astype(BF16)) for p in pairs]
    tm = [lv[nlev - 1] + x * lv[0] for x in a_ab]
    for lev in range(1, nlev - 1):
        step = [_mm_split_lhs(tm[p], a_ab[p] * lv[lev]) for p in pairs]
        tm = [tm[p] + _mm_split_lhs(step[p], tm[p]) for p in pairs]
    wu = [_mm_split_both(tm[p], jnp.concatenate([at_e[p], akv[p]], axis=1)) for p in pairs]
    rhs4 = [jnp.concatenate([wu[p], jnp.concatenate([jnp.zeros_like(v_e[p]), v_e[p]], axis=1)],
                            axis=0).astype(BF16) for p in pairs]
    o4 = [_dot(m_r[p], rhs4[p]) for p in pairs]
    q = [fold(rt_e[p] + o4[p][:, 0:LANES]).astype(BF16) for p in pairs]
    y1 = [fold(o4[p][:, LANES:2 * LANES]) for p in pairs]
    rhs5 = [jnp.concatenate([expand(b[p] * e_hat[p]), expand(k[p] * e_hat[p])], axis=0).astype(BF16)
            for p in pairs]
    o5 = [lax.dot_general(rhs4[p], rhs5[p], (((0,), (0,)), ((), ())), preferred_element_type=F32)
          for p in pairs]
    s_bf = [x.astype(BF16) for x in s_old]
    y = [lax.dot_general(q[p], s_bf[p], nt, preferred_element_type=F32) + y1[p] for p in pairs]
    s_new = [s_old[p] * jnp.exp(cum_l[p]) + _dot(s_bf[p], o5[p][0:n].astype(BF16)) + o5[p][n:2 * n]
             for p in pairs]
    for p in pairs:
        y_ref[:, sls[p]] = y[p]
        st_ref[0, p] = fold(s_new[p])


def _wkv_chunk(r, lw, k, v, a, b, s0, t):
    rows, c = r.shape
    s = rows // t
    npair = c // LANES
    nc = t // CHUNK
    g = _pick(npair, (8, 4, 2, 1))
    lv, m1, tri = _chunk_masks()
    seq = pl.BlockSpec((CHUNK, g * LANES), lambda si, pi, ci: (si * nc + ci, pi))
    st = pl.BlockSpec((1, g, HEAD, LANES), lambda si, pi, ci: (si, pi, 0, 0))
    const = lambda x: pl.BlockSpec(x.shape, lambda si, pi, ci: (0,) * x.ndim)
    return pl.pallas_call(
        functools.partial(_wkv_chunk_kernel, g=g),
        grid=(s, npair // g, nc),
        in_specs=[seq] * 6 + [st, const(lv), const(m1), const(tri)],
        out_specs=[seq, st],
        out_shape=[jax.ShapeDtypeStruct((rows, c), F32),
                   jax.ShapeDtypeStruct((s, npair, HEAD, LANES), F32)],
        compiler_params=_params(("parallel", "parallel", "arbitrary")),
        name="wkv_chunk",
    )(r, lw, k, v, a, b, s0, lv, m1, tri)


def _rwkv_post_kernel(y_ref, bn_ref, v_ref, g_ref, gg_ref, gb_ref, bd_ref, o_ref):
    y = y_ref[...]
    bd = bd_ref[...]
    ym = _head_sum(y, bd) * (1.0 / HEAD)
    yc = y - ym
    yv = _head_sum(yc * yc, bd) * (1.0 / HEAD)
    yn = yc * lax.rsqrt(yv + GN_EPS) * gg_ref[...] + gb_ref[...]
    o_ref[...] = ((yn + bn_ref[...] * v_ref[...]) * g_ref[...]).astype(o_ref.dtype)


def _rwkv_post(y, bn, v, g, gn_g, gn_b, bd):
    m, c = y.shape
    tm = _pick(m, (256, 128, 64, 32, 16, 8))
    row = pl.BlockSpec((tm, c), lambda i: (i, 0))
    vec = pl.BlockSpec((1, c), lambda i: (0, 0))
    return pl.pallas_call(
        _rwkv_post_kernel,
        grid=(m // tm,),
        in_specs=[row, row, row, row, vec, vec, pl.BlockSpec((LANES, LANES), lambda i: (0, 0))],
        out_specs=row,
        out_shape=jax.ShapeDtypeStruct((m, c), BF16),
        compiler_params=_params(("parallel",)),
        name="rwkv_post",
    )(y, bn, v, g, gn_g, gn_b, bd)


def _pack_state(s):
    n, h = s.shape[0], s.shape[1]
    return s.reshape(n, h // 2, 2, HEAD, HEAD).transpose(0, 1, 3, 2, 4).reshape(n, h // 2, HEAD, 2 * HEAD)


def _unpack_state(s):
    n, hp = s.shape[0], s.shape[1]
    return s.reshape(n, hp, HEAD, 2, HEAD).transpose(0, 1, 3, 2, 4).reshape(n, 2 * hp, HEAD, HEAD)


def _layer(x_prompt, x_sample, wkv0, conv0, shift0,
           ln_mix_pre, ln_mix_post, ln_ffn_pre, ln_ffn_post, w_in, b_gate,
           conv_w, conv_b, conv_ln_g, conv_ln_b, w_conv_out, shift_mu,
           w0, w2, a0, a2, g2, k_k, k_a, r_k, gn_g, gn_b, w_rwkv_out, w_o,
           w_ffn_gate, w_ffn_up, w_ffn_down):
    bp, tp, d = x_prompt.shape
    bs, ts, _ = x_sample.shape
    c = conv_w.shape[1]
    kw = conv_w.shape[0]
    dr = w2.shape[0]
    sw = shift_mu.shape[0]
    mp, ms = bp * tp, bs * ts
    assert tp % CHUNK == 0 and tp >= kw - 1, "prompt sequences are processed in 64-token chunks"
    tm = _pick(math.gcd(mp, ms), (512, 256, 128, 64, 32, 16))
    row = lambda x: x.reshape(1, -1)

    w_g0 = w_in[:, 2 * c + sw:2 * c + sw + d].astype(BF16)
    w_g1 = w_in[:, 2 * c + sw + d:].astype(BF16)
    zeros = jnp.zeros((dr, c), F32)
    wda = jnp.concatenate([jnp.concatenate([w2, zeros], axis=1),
                           jnp.concatenate([zeros, a2], axis=1)], axis=0).astype(BF16)
    idx = jnp.arange(LANES) // HEAD
    bd = (idx[:, None] == idx[None, :]).astype(BF16)

    x_p = x_prompt.reshape(mp, d)
    x_s = x_sample.reshape(ms, d)
    h = _rms_cast(x_p, x_s, row(ln_mix_pre), min(tm, 256))

    conv_args = (conv_w, row(conv_b), row(conv_ln_g), row(conv_ln_b))
    u = _mm_pair(_mm_glu_kernel, h, w_in, 0, w_in, c, c, F32, "mm_glu")
    u_s = u[mp:].reshape(bs, ts, c)
    zc_p = _conv_seq(u, bp, tp, *conv_args)
    ext_s = jnp.concatenate([conv0, u_s], axis=1)
    zc_s = _conv_step(ext_s.transpose(1, 0, 2), *conv_args).transpose(1, 0, 2).reshape(ms, c)
    conv_p = jnp.stack([u[(q + 1) * tp - (kw - 1):(q + 1) * tp] for q in range(bp)])
    conv_s = ext_s[:, ts:]

    pr = _mm(h, w_in, sw, 2 * c, name="mm_shift")
    pr_s = pr[mp:].reshape(bs, ts, sw)
    shifted_s = jnp.concatenate([shift0[:, None], pr_s[:, :-1]], axis=1).reshape(ms, sw)
    consts = (row(shift_mu), row(w0), row(a0), wda, g2.astype(BF16), row(k_k), row(k_a), row(r_k), bd)
    vec_p = _rwkv_prep(pr, None, mp, tp, consts, c, dr)
    vec_s = _rwkv_prep(pr_s.reshape(ms, sw), shifted_s, ms, ts, consts, c, dr)
    y_p, st_p = _wkv_chunk(*vec_p[:6], jnp.zeros((bp, c // LANES, HEAD, LANES), F32), tp)
    y_s, st_s = _wkv_step(*vec_s[:6], _pack_state(wkv0), bd, ts)
    yr_p = _rwkv_post(y_p, vec_p[7], vec_p[3], vec_p[6], row(gn_g), row(gn_b), bd)
    yr_s = _rwkv_post(y_s, vec_s[7], vec_s[3], vec_s[6], row(gn_g), row(gn_b), bd)
    shift_p = pr[tp - 1:mp:tp]
    shift_s = pr_s[:, -1]

    mixed = _merge(h, zc_p, zc_s, yr_p, yr_s, w_g0, w_g1, w_conv_out.astype(BF16),
                   w_rwkv_out.astype(BF16), b_gate[0:1], b_gate[1:2], tm)
    o = _mm(mixed, w_o, name="mm_o")
    x1, h2 = _rms_res(x_p, x_s, o, row(ln_mix_post), row(ln_ffn_pre), min(tm, 256))

    act = _mm_pair(_mm_swiglu_kernel, h2, w_ffn_gate, 0, w_ffn_up, 0, w_ffn_gate.shape[1], BF16, "mm_swiglu")
    f = _mm(act, w_ffn_down.astype(BF16), name="mm_down")
    tl = min(tm, 256)
    y_prompt = _rms_res_last(x1, f, row(ln_ffn_post), 0, mp, tl)
    y_sample = _rms_res_last(x1, f, row(ln_ffn_post), mp, ms, tl)

    return (y_prompt.reshape(bp, tp, d), y_sample.reshape(bs, ts, d),
            _unpack_state(st_p), conv_p, shift_p,
            _unpack_state(st_s), conv_s, shift_s)


def kernel(x_prompt, x_sample, state_wkv, state_conv, state_shift, ln_mix_pre, ln_mix_post, ln_ffn_pre,
           ln_ffn_post, w_in, b_gate, conv_w, conv_b, conv_ln_g, conv_ln_b, w_conv_out, shift_mu, w0, w2,
           a0, a2, g2, k_k, k_a, r_k, gn_g, gn_b, w_rwkv_out, w_o, w_ffn_gate, w_ffn_up, w_ffn_down):
    depth = w_in.shape[0]
    assert depth == 1, "one decoder layer per step"
    weights = (ln_mix_pre, ln_mix_post, ln_ffn_pre, ln_ffn_post, w_in, b_gate,
               conv_w, conv_b, conv_ln_g, conv_ln_b, w_conv_out, shift_mu,
               w0, w2, a0, a2, g2, k_k, k_a, r_k, gn_g, gn_b, w_rwkv_out, w_o,
               w_ffn_gate, w_ffn_up, w_ffn_down)
    lw = tuple(wt[0] for wt in weights)
    yp, ys, wkv_p, conv_p, shift_p, wkv_s, conv_s, shift_s = _layer(
        x_prompt, x_sample, state_wkv[0], state_conv[0], state_shift[0], *lw)
    return (yp, ys, wkv_p[None], conv_p[None], shift_p[None],
            wkv_s[None], conv_s[None], shift_s[None])
```

```python
import functools
import math

import jax
import jax.numpy as jnp
from jax import lax
from jax.experimental import pallas as pl
from jax.experimental.pallas import tpu as pltpu

F32 = jnp.float32
BF16 = jnp.bfloat16

RMS_EPS = 1e-6
LN_EPS = 1e-5
GN_EPS = 64e-5
HEAD = 64
LANES = 128
SUBLANES = 8
CONV_HALO = 32
CHUNK = 64
VMEM_LIMIT = 56 * 1024 * 1024


def _pick(n, cands):
    for c in cands:
        if n % c == 0:
            return c
    return n


def _params(sem):
    return pltpu.CompilerParams(dimension_semantics=sem, vmem_limit_bytes=VMEM_LIMIT)


def _dot(a, b):
    return jnp.dot(a, b, preferred_element_type=F32)


def _split(x):
    hi = x.astype(BF16)
    return hi, (x - hi.astype(F32)).astype(BF16)


def _mm_split_lhs(a, b):
    hi, lo = _split(a)
    bb = b.astype(BF16)
    return _dot(hi, bb) + _dot(lo, bb)


def _mm_split_both(a, b):
    ah, al = _split(a)
    bh, bl = _split(b)
    return _dot(ah, bh) + _dot(al, bh) + _dot(ah, bl)


def _head_sum(x, bd):
    cols = [_dot(x[:, c:c + LANES].astype(BF16), bd) for c in range(0, x.shape[1], LANES)]
    return jnp.concatenate(cols, axis=1)


def _two_group_specs(tm, d, nblk_p):
    return (pl.BlockSpec((tm, d), lambda i, *_: (jnp.minimum(i, nblk_p - 1), 0)),
            pl.BlockSpec((tm, d), lambda i, *_: (jnp.maximum(i - nblk_p, 0), 0)))


def _two_group_rows(p_ref, s_ref, nblk_p):
    return jnp.where(pl.program_id(0) < nblk_p, p_ref[...], s_ref[...])


def _rms_cast_kernel(xp_ref, xs_ref, g_ref, o_ref, *, nblk_p):
    x = _two_group_rows(xp_ref, xs_ref, nblk_p)
    ms = jnp.mean(x * x, axis=-1, keepdims=True)
    o_ref[...] = (x * lax.rsqrt(ms + RMS_EPS) * g_ref[...]).astype(o_ref.dtype)


def _rms_cast(x_p, x_s, g, tm):
    (mp, d), ms = x_p.shape, x_s.shape[0]
    xp_spec, xs_spec = _two_group_specs(tm, d, mp // tm)
    return pl.pallas_call(
        functools.partial(_rms_cast_kernel, nblk_p=mp // tm),
        grid=((mp + ms) // tm,),
        in_specs=[xp_spec, xs_spec, pl.BlockSpec((1, d), lambda i: (0, 0))],
        out_specs=pl.BlockSpec((tm, d), lambda i: (i, 0)),
        out_shape=jax.ShapeDtypeStruct((mp + ms, d), BF16),
        compiler_params=_params(("parallel",)),
        name="rms_cast",
    )(x_p, x_s, g)


def _rms_res_kernel(xp_ref, xs_ref, o_ref, g_ref, g2_ref, x1_ref, h_ref, *, nblk_p):
    o = o_ref[...]
    ms = jnp.mean(o * o, axis=-1, keepdims=True)
    x1 = _two_group_rows(xp_ref, xs_ref, nblk_p) + o * lax.rsqrt(ms + RMS_EPS) * g_ref[...]
    x1_ref[...] = x1
    ms1 = jnp.mean(x1 * x1, axis=-1, keepdims=True)
    h_ref[...] = (x1 * lax.rsqrt(ms1 + RMS_EPS) * g2_ref[...]).astype(h_ref.dtype)


def _rms_res(x_p, x_s, o, g, g2, tm):
    (mp, d), m = x_p.shape, o.shape[0]
    xp_spec, xs_spec = _two_group_specs(tm, d, mp // tm)
    row = pl.BlockSpec((tm, d), lambda i: (i, 0))
    vec = pl.BlockSpec((1, d), lambda i: (0, 0))
    return pl.pallas_call(
        functools.partial(_rms_res_kernel, nblk_p=mp // tm),
        grid=(m // tm,),
        in_specs=[xp_spec, xs_spec, row, vec, vec],
        out_specs=[row, row],
        out_shape=[jax.ShapeDtypeStruct((m, d), F32), jax.ShapeDtypeStruct((m, d), BF16)],
        compiler_params=_params(("parallel",)),
        name="rms_res",
    )(x_p, x_s, o, g, g2)


def _rms_res_last_kernel(x_ref, o_ref, g_ref, x1_ref):
    o = o_ref[...]
    ms = jnp.mean(o * o, axis=-1, keepdims=True)
    x1_ref[...] = x_ref[...] + o * lax.rsqrt(ms + RMS_EPS) * g_ref[...]


def _rms_res_last(x, o, g, row0, rows, tm):
    d = x.shape[1]
    blk0 = row0 // tm
    src = pl.BlockSpec((tm, d), lambda i: (blk0 + i, 0))
    return pl.pallas_call(
        _rms_res_last_kernel,
        grid=(rows // tm,),
        in_specs=[src, src, pl.BlockSpec((1, d), lambda i: (0, 0))],
        out_specs=pl.BlockSpec((tm, d), lambda i: (i, 0)),
        out_shape=jax.ShapeDtypeStruct((rows, d), F32),
        compiler_params=_params(("parallel",)),
        name="rms_res_last",
    )(x, o, g)


MM_VMEM_BUDGET = 50 * 1024 * 1024


def _mm_tiles(m, k, n, col0s, w_bytes, n_w, out_bytes):
    span = math.gcd(n, *col0s)
    tns = [t for t in (512, 256, 128) if span % t == 0] or [span]
    for tm in (1088, 1024, 544, 512, 256, 128, 64, 32, 16, 8):
        for tn in tns:
            need = (2 * tm * k * 2
                    + n_w * k * tn * (2 * w_bytes + (2 if w_bytes > 2 else 0))
                    + tm * tn * (2 * out_bytes + 4 * n_w))
            if m % tm == 0 and need <= MM_VMEM_BUDGET:
                return tm, tn
    return m, tns[-1]


def _mm_kernel(x_ref, w_ref, o_ref):
    o_ref[...] = _dot(x_ref[...], w_ref[...].astype(BF16)).astype(o_ref.dtype)


def _mm(x, w, n=None, col0=0, out_dtype=F32, name="mm"):
    m, k = x.shape
    n = w.shape[1] if n is None else n
    tm, tn = _mm_tiles(m, k, n, (col0,), w.dtype.itemsize, 1, jnp.dtype(out_dtype).itemsize)
    j0 = col0 // tn
    return pl.pallas_call(
        _mm_kernel,
        grid=(m // tm, n // tn),
        in_specs=[pl.BlockSpec((tm, k), lambda i, j: (i, 0)),
                  pl.BlockSpec((k, tn), lambda i, j: (0, j0 + j))],
        out_specs=pl.BlockSpec((tm, tn), lambda i, j: (i, j)),
        out_shape=jax.ShapeDtypeStruct((m, n), out_dtype),
        compiler_params=_params(("parallel", "arbitrary")),
        name=name,
    )(x, w)


def _mm_glu_kernel(x_ref, wa_ref, wb_ref, o_ref):
    x = x_ref[...]
    o_ref[...] = _dot(x, wa_ref[...].astype(BF16)) * jax.nn.sigmoid(_dot(x, wb_ref[...].astype(BF16)))


def _mm_swiglu_kernel(x_ref, wg_ref, wu_ref, o_ref):
    x = x_ref[...]
    o_ref[...] = (jax.nn.silu(_dot(x, wg_ref[...].astype(BF16)))
                  * _dot(x, wu_ref[...].astype(BF16))).astype(o_ref.dtype)


def _mm_pair(kern, x, wa, col_a, wb, col_b, n, out_dtype, name):
    m, k = x.shape
    tm, tn = _mm_tiles(m, k, n, (col_a, col_b), wa.dtype.itemsize, 2, jnp.dtype(out_dtype).itemsize)
    ja, jb = col_a // tn, col_b // tn
    return pl.pallas_call(
        kern,
        grid=(m // tm, n // tn),
        in_specs=[pl.BlockSpec((tm, k), lambda i, j: (i, 0)),
                  pl.BlockSpec((k, tn), lambda i, j: (0, ja + j)),
                  pl.BlockSpec((k, tn), lambda i, j: (0, jb + j))],
        out_specs=pl.BlockSpec((tm, tn), lambda i, j: (i, j)),
        out_shape=jax.ShapeDtypeStruct((m, n), out_dtype),
        compiler_params=_params(("parallel", "arbitrary")),
        name=name,
    )(x, wa, wb)


def _merge_kernel(h_ref, zcp_ref, zcs_ref, yrp_ref, yrs_ref, wg0_ref, wg1_ref, wc_ref, wr_ref,
                  b0_ref, b1_ref, o_ref, *, nblk_p):
    h = h_ref[...]
    g0 = jax.nn.sigmoid(_dot(h, wg0_ref[...]) + b0_ref[...])
    g1 = jax.nn.sigmoid(_dot(h, wg1_ref[...]) + b1_ref[...])
    oc = _dot(_two_group_rows(zcp_ref, zcs_ref, nblk_p), wc_ref[...])
    orr = _dot(_two_group_rows(yrp_ref, yrs_ref, nblk_p), wr_ref[...])
    o_ref[...] = (g0 * oc + g1 * orr).astype(o_ref.dtype)


def _merge(h, zc_p, zc_s, yr_p, yr_s, wg0, wg1, wc, wr, b0, b1, tm):
    m, d = h.shape
    mp, c = zc_p.shape
    tn = _pick(d, (256, 128))
    gp_spec, gs_spec = _two_group_specs(tm, c, mp // tm)
    return pl.pallas_call(
        functools.partial(_merge_kernel, nblk_p=mp // tm),
        grid=(m // tm, d // tn),
        in_specs=[pl.BlockSpec((tm, d), lambda i, j: (i, 0)),
                  gp_spec, gs_spec, gp_spec, gs_spec,
                  pl.BlockSpec((d, tn), lambda i, j: (0, j)),
                  pl.BlockSpec((d, tn), lambda i, j: (0, j)),
                  pl.BlockSpec((c, tn), lambda i, j: (0, j)),
                  pl.BlockSpec((c, tn), lambda i, j: (0, j)),
                  pl.BlockSpec((1, tn), lambda i, j: (0, j)),
                  pl.BlockSpec((1, tn), lambda i, j: (0, j))],
        out_specs=pl.BlockSpec((tm, tn), lambda i, j: (i, j)),
        out_shape=jax.ShapeDtypeStruct((m, d), BF16),
        compiler_params=_params(("parallel", "arbitrary")),
        name="merge",
    )(h, zc_p, zc_s, yr_p, yr_s, wg0, wg1, wc, wr, b0, b1)


def _ln_silu(z, g, b):
    mu = jnp.mean(z, axis=-1, keepdims=True)
    zc = z - mu
    var = jnp.mean(zc * zc, axis=-1, keepdims=True)
    y = zc * lax.rsqrt(var + LN_EPS) * g + b
    return y * jax.nn.sigmoid(y)


def _conv_seq_kernel(um_ref, uh_ref, w_ref, cb_ref, g_ref, b_ref, o_ref, x_scr, xs_scr, z_scr,
                     *, tb, kw, lc):
    i = pl.program_id(1)
    c = um_ref.shape[1]
    halo = uh_ref[...]
    x_scr[0:CONV_HALO, :] = jnp.where(i == 0, jnp.zeros_like(halo), halo)
    x_scr[CONV_HALO:CONV_HALO + tb, :] = um_ref[...]
    off = CONV_HALO - (kw - 1)
    first = True
    for s in range(SUBLANES):
        taps = [j for j in range(kw) if (off + j) % SUBLANES == s]
        if not taps:
            continue
        span = max(off + j - s for j in taps) + tb
        xs_scr[0:span, :] = x_scr[s:s + span, :]
        for c0 in range(0, c, lc):
            cols = slice(c0, c0 + lc)
            acc = cb_ref[:, cols] if first else z_scr[:, cols]
            for j in taps:
                a0 = off + j - s
                acc = acc + xs_scr[a0:a0 + tb, cols] * w_ref[j:j + 1, cols]
            z_scr[:, cols] = acc
        first = False
    o_ref[...] = _ln_silu(z_scr[...], g_ref[...], b_ref[...]).astype(o_ref.dtype)


def _conv_seq(u, nseq, t, conv_w, conv_b, ln_g, ln_b):
    c = u.shape[1]
    kw = conv_w.shape[0]
    tb = _pick(t, (64, 32))
    lc = _pick(c, (256, 128))
    nb = t // tb
    hb = tb // CONV_HALO
    vec = pl.BlockSpec((1, c), lambda bi, i: (0, 0))
    return pl.pallas_call(
        functools.partial(_conv_seq_kernel, tb=tb, kw=kw, lc=lc),
        grid=(nseq, nb),
        in_specs=[pl.BlockSpec((tb, c), lambda bi, i: (bi * nb + i, 0)),
                  pl.BlockSpec((CONV_HALO, c), lambda bi, i: (jnp.maximum((bi * nb + i) * hb - 1, 0), 0)),
                  pl.BlockSpec((kw, c), lambda bi, i: (0, 0)),
                  vec, vec, vec],
        out_specs=pl.BlockSpec((tb, c), lambda bi, i: (bi * nb + i, 0)),
        out_shape=jax.ShapeDtypeStruct((nseq * t, c), BF16),
        scratch_shapes=[pltpu.VMEM((CONV_HALO + tb, c), F32), pltpu.VMEM((CONV_HALO + tb, c), F32),
                        pltpu.VMEM((tb, c), F32)],
        compiler_params=_params(("parallel", "arbitrary")),
        name="conv_seq",
    )(u, u, conv_w, conv_b, ln_g, ln_b)


def _conv_step_kernel(e_ref, w_ref, cb_ref, g_ref, b_ref, o_ref, *, kw, nt):
    for t in range(nt):
        acc = e_ref[t] * w_ref[0:1, :]
        for j in range(1, kw):
            acc = acc + e_ref[t + j] * w_ref[j:j + 1, :]
        o_ref[t] = _ln_silu(acc + cb_ref[...], g_ref[...], b_ref[...]).astype(o_ref.dtype)


def _conv_step(ext_tm, conv_w, conv_b, ln_g, ln_b):
    te, b, c = ext_tm.shape
    kw = conv_w.shape[0]
    nt = te - (kw - 1)
    sb = _pick(b, (8,))
    vec = pl.BlockSpec((1, c), lambda i: (0, 0))
    return pl.pallas_call(
        functools.partial(_conv_step_kernel, kw=kw, nt=nt),
        grid=(b // sb,),
        in_specs=[pl.BlockSpec((te, sb, c), lambda i: (0, i, 0)),
                  pl.BlockSpec((kw, c), lambda i: (0, 0)),
                  vec, vec, vec],
        out_specs=pl.BlockSpec((nt, sb, c), lambda i: (0, i, 0)),
        out_shape=jax.ShapeDtypeStruct((nt, b, c), BF16),
        compiler_params=_params(("parallel",)),
        name="conv_step",
    )(ext_tm, conv_w, conv_b, ln_g, ln_b)


def _rwkv_vector_math(m_r, m_k, m_v, m_low, w0, a0, wd_w, wd_a, g2, k_k, k_a, r_k, bd, dr):
    low = m_low[:, 0:2 * dr]
    lane = lax.broadcasted_iota(jnp.int32, low.shape, 1)
    low = jnp.where(lane < dr, jnp.tanh(low), low).astype(BF16)
    logw = -math.exp(-0.5) * jax.nn.sigmoid(w0 + _dot(low, wd_w))
    rate = jax.nn.sigmoid(a0 + _dot(low, wd_a))
    gate = _dot(jax.nn.sigmoid(m_low[:, 2 * dr:]).astype(BF16), g2)
    kk = m_k * k_k
    kk = kk / jnp.maximum(jnp.sqrt(_head_sum(kk * kk, bd)), 1e-12)
    k2 = m_k * (1.0 + (rate - 1.0) * k_a)
    bonus = _head_sum(m_r * k2 * r_k, bd)
    return m_r, logw, k2, m_v, -kk, kk * rate, gate, bonus


def _rwkv_prep_kernel(pr_ref, sh_ref, mu_ref, w0_ref, a0_ref, wda_ref, g2_ref, kk_ref, ka_ref, rk_ref,
                      bd_ref, *outs, c, dr):
    pr = pr_ref[...]
    m = pr + (sh_ref[...] - pr) * mu_ref[...]
    vals = _rwkv_vector_math(m[:, 0:c], m[:, c:2 * c], m[:, 2 * c:3 * c], m[:, 3 * c:],
                             w0_ref[...], a0_ref[...], wda_ref[:, 0:c], wda_ref[:, c:2 * c], g2_ref[...],
                             kk_ref[...], ka_ref[...], rk_ref[...], bd_ref[...], dr)
    for o_ref, val in zip(outs, vals):
        o_ref[...] = val


def _rwkv_prep(pr, shifted, consts, c, dr):
    rows, sw = pr.shape
    tm = _pick(rows, (128, 64, 32, 16, 8))
    row = pl.BlockSpec((tm, sw), lambda i: (i, 0))
    out = pl.BlockSpec((tm, c), lambda i: (i, 0))
    full = lambda x: pl.BlockSpec(x.shape, lambda i: (0,) * x.ndim)
    return pl.pallas_call(
        functools.partial(_rwkv_prep_kernel, c=c, dr=dr),
        grid=(rows // tm,),
        in_specs=[row, row] + [full(x) for x in consts],
        out_specs=[out] * 8,
        out_shape=[jax.ShapeDtypeStruct((rows, c), F32)] * 8,
        compiler_params=_params(("parallel",)),
        name="rwkv_prep",
    )(pr, shifted, *consts)


def _wkv_step_kernel(r_ref, w_ref, k_ref, v_ref, a_ref, b_ref, s0_ref, bd_ref, y_ref, st_ref,
                     *, nseq, t, g):
    bd = bd_ref[...]
    rows = lax.broadcasted_iota(jnp.int32, (HEAD, LANES), 0)
    lanes = lax.broadcasted_iota(jnp.int32, (HEAD, LANES), 1)
    diag = jnp.where(rows == lanes % HEAD, 1.0, 0.0).astype(F32)
    chains = [(q, p) for q in range(nseq) for p in range(g)]
    rr, kk, vv, aa, bb = (ref[...] for ref in (r_ref, k_ref, v_ref, a_ref, b_ref))
    ww = jnp.exp(w_ref[...])
    s = [jnp.concatenate([s0_ref[q, 2 * p], s0_ref[q, 2 * p + 1]], axis=1) for q, p in chains]
    yrows = {}
    for i in range(t):
        bc = lambda x, q, p: jnp.broadcast_to(
            x[q * t + i:q * t + i + 1, p * LANES:(p + 1) * LANES], (HEAD, LANES))
        lhs = jnp.concatenate(
            [jnp.concatenate([s[n] * bc(aa, q, p), bc(vv, q, p) * diag], axis=0)
             for n, (q, p) in enumerate(chains)], axis=0)
        res = _dot(lhs.astype(BF16), bd)
        s = [s[n] * bc(ww, q, p) + res[2 * n * HEAD:(2 * n + 1) * HEAD] * bc(bb, q, p)
             + res[(2 * n + 1) * HEAD:(2 * n + 2) * HEAD] * bc(kk, q, p)
             for n, (q, p) in enumerate(chains)]
        sr = jnp.concatenate([s[n] * bc(rr, q, p) for n, (q, p) in enumerate(chains)], axis=0)
        yc = _dot(sr.astype(BF16), bd)
        for n, (q, p) in enumerate(chains):
            yrows[(q, i, p)] = jnp.sum(yc[n * HEAD:(n + 1) * HEAD] * diag, axis=0, keepdims=True)
    y_ref[...] = jnp.concatenate(
        [jnp.concatenate([yrows[(q, i, p)] for p in range(g)], axis=1)
         for q in range(nseq) for i in range(t)], axis=0)
    for n, (q, p) in enumerate(chains):
        st_ref[q, 2 * p] = s[n][:, 0:HEAD]
        st_ref[q, 2 * p + 1] = s[n][:, HEAD:2 * HEAD]


def _wkv_step(r, lw, k, v, a, b, s0, bd, t):
    rows, c = r.shape
    npair = c // LANES
    assert SUBLANES % t == 0, "token-by-token path expects a few new tokens per sequence"
    nseq = SUBLANES // t
    g = _pick(npair, (8, 4, 2, 1))
    seq = pl.BlockSpec((nseq * t, g * LANES), lambda si, pi: (si, pi))
    st = pl.BlockSpec((nseq, 2 * g, HEAD, HEAD), lambda si, pi: (si, pi, 0, 0))
    return pl.pallas_call(
        functools.partial(_wkv_step_kernel, nseq=nseq, t=t, g=g),
        grid=(rows // (nseq * t), npair // g),
        in_specs=[seq] * 6 + [st, pl.BlockSpec((LANES, LANES), lambda si, pi: (0, 0))],
        out_specs=[seq, st],
        out_shape=[jax.ShapeDtypeStruct((rows, c), F32),
                   jax.ShapeDtypeStruct(s0.shape, F32)],
        compiler_params=_params(("parallel", "parallel")),
        name="wkv_step",
    )(r, lw, k, v, a, b, s0, bd)


def _chunk_masks():
    n = 2 * CHUNK
    ri = jnp.arange(n)[:, None]
    ci = jnp.arange(n)[None, :]
    levels = [(ri // 2) == (ci // 2)]
    bsz = 2
    while bsz < CHUNK:
        levels.append(((ri // (2 * bsz)) == (ci // (2 * bsz))) & ((ri // bsz) != (ci // bsz)))
        bsz *= 2
    levels.append(ri == ci)
    lv = jnp.stack(levels).astype(F32)
    r4 = jnp.arange(2 * n)[:, None]
    c4 = jnp.arange(2 * n)[None, :]
    same_head = ((r4 // CHUNK) % 2) == ((c4 // CHUNK) % 2)
    t, s = r4 % CHUNK, c4 % CHUNK
    m1 = jnp.where(r4 < n, same_head & (s < t), same_head & (s <= t)).astype(F32)
    tri = (jnp.arange(CHUNK)[:, None] >= jnp.arange(CHUNK)[None, :]).astype(BF16)
    return lv, m1, tri


def _post_math(y, bonus, v, gate, gn_g, gn_b, bd):
    ym = _head_sum(y, bd) * (1.0 / HEAD)
    yc = y - ym
    yv = _head_sum(yc * yc, bd) * (1.0 / HEAD)
    yn = yc * lax.rsqrt(yv + GN_EPS) * gn_g + gn_b
    return (yn + bonus * v) * gate


def _rwkv_chunk_kernel(pr_r, pr_k, pr_v, pr_l, hr_ref, hk_ref, hv_ref, hl_ref, mu_r, mu_k, mu_v, mu_l,
                       w0_ref, a0_ref, wdw_ref, wda_ref, g2_ref, kk_ref, ka_ref, rk_ref, gg_ref, gb_ref,
                       s0_ref, lv_ref, m1_ref, tri_ref, bd_ref, y_ref, st_ref, *, g, dr):
    ci = pl.program_id(2)
    n = 2 * CHUNK
    lane = lax.broadcasted_iota(jnp.int32, (1, LANES), 1)
    m0 = jnp.where(lane < HEAD, 1.0, 0.0).astype(F32)
    m1 = 1.0 - m0
    expand = lambda x: jnp.concatenate([x * m0, x * m1], axis=0)
    fold = lambda x: x[0:CHUNK] + x[CHUNK:n]

    @pl.when(ci == 0)
    def _():
        st_ref[...] = s0_ref[...]

    def shifted_lerp(x_ref, h_ref, mu_ref):
        x = x_ref[...]
        prev = h_ref[SUBLANES - 1:SUBLANES, :]
        prev = jnp.where(ci == 0, jnp.zeros_like(prev), prev)
        row = lax.broadcasted_iota(jnp.int32, x.shape, 0)
        shifted = jnp.where(row == 0, jnp.broadcast_to(prev, x.shape), pltpu.roll(x, 1, 0))
        return x + (shifted - x) * mu_ref[...]

    bd = bd_ref[...]
    r_all, lw_all, k_all, v_all, a_all, b_all, gate, bonus = _rwkv_vector_math(
        shifted_lerp(pr_r, hr_ref, mu_r), shifted_lerp(pr_k, hk_ref, mu_k), shifted_lerp(pr_v, hv_ref, mu_v),
        shifted_lerp(pr_l, hl_ref, mu_l), w0_ref[...], a0_ref[...], wdw_ref[...], wda_ref[...], g2_ref[...],
        kk_ref[...], ka_ref[...], rk_ref[...], bd, dr)

    pairs = range(g)
    tri = tri_ref[...]
    mask1 = m1_ref[...]
    nlev = lv_ref.shape[0]
    lv = [lv_ref[i] for i in range(nlev)]
    sls = [slice(p * LANES, (p + 1) * LANES) for p in pairs]
    r, lw, k, v, a, b = ([x[:, sl] for sl in sls] for x in (r_all, lw_all, k_all, v_all, a_all, b_all))
    s_old = [expand(st_ref[0, p]) for p in pairs]
    lw_split = [_split(x) for x in lw]
    cum = [_dot(tri, hi) + _dot(tri, lo) for hi, lo in lw_split]
    cum_l = [x[CHUNK - 1:CHUNK] for x in cum]
    e_neg = [jnp.exp(-x) for x in cum]
    e_hat = [jnp.exp(cl - x) for cl, x in zip(cum_l, cum)]
    at_e = [expand(a[p] * jnp.exp(cum[p] - lw[p])) for p in pairs]
    rt_e = [expand(r[p] * jnp.exp(cum[p])) for p in pairs]
    v_e = [expand(x) for x in v]
    bt = [(b[p] * e_neg[p]).astype(BF16) for p in pairs]
    kt = [(k[p] * e_neg[p]).astype(BF16) for p in pairs]
    nt = (((1,), (1,)), ((), ()))
    out1 = [lax.dot_general(jnp.concatenate([at_e[p], rt_e[p]], axis=0).astype(BF16),
                            jnp.concatenate([bt[p], bt[p], kt[p], kt[p]], axis=0), nt,
                            preferred_element_type=F32) for p in pairs]
    out1 = [jnp.where(mask1 != 0.0, x, 0.0) for x in out1]
    a_ab = [x[0:n, 0:n] for x in out1]
    m_r = [x[n:2 * n, :].astype(BF16) for x in out1]
    akv = [_dot(out1[p][0:n, n:2 * n].astype(BF16), v_e[p].astype(BF16)) for p in pairs]
    tm = [lv[nlev - 1] + x * lv[0] for x in a_ab]
    for lev in range(1, nlev - 1):
        step = [_mm_split_lhs(tm[p], a_ab[p] * lv[lev]) for p in pairs]
        tm = [tm[p] + _mm_split_lhs(step[p], tm[p]) for p in pairs]
    wu = [_mm_split_both(tm[p], jnp.concatenate([at_e[p], akv[p]], axis=1)) for p in pairs]
    rhs4 = [jnp.concatenate([wu[p], jnp.concatenate([jnp.zeros_like(v_e[p]), v_e[p]], axis=1)],
                            axis=0).astype(BF16) for p in pairs]
    o4 = [_dot(m_r[p], rhs4[p]) for p in pairs]
    q = [fold(rt_e[p] + o4[p][:, 0:LANES]).astype(BF16) for p in pairs]
    y1 = [fold(o4[p][:, LANES:2 * LANES]) for p in pairs]
    rhs5 = [jnp.concatenate([expand(b[p] * e_hat[p]), expand(k[p] * e_hat[p])], axis=0).astype(BF16)
            for p in pairs]
    o5 = [lax.dot_general(rhs4[p], rhs5[p], (((0,), (0,)), ((), ())), preferred_element_type=F32)
          for p in pairs]
    s_bf = [x.astype(BF16) for x in s_old]
    y = [lax.dot_general(q[p], s_bf[p], nt, preferred_element_type=F32) + y1[p] for p in pairs]
    s_new = [s_old[p] * jnp.exp(cum_l[p]) + _dot(s_bf[p], o5[p][0:n].astype(BF16)) + o5[p][n:2 * n]
             for p in pairs]
    y_all = jnp.concatenate(y, axis=1)
    y_ref[...] = _post_math(y_all, bonus, v_all, gate, gg_ref[...], gb_ref[...], bd).astype(y_ref.dtype)
    for p in pairs:
        st_ref[0, p] = fold(s_new[p])


def _rwkv_chunk(pr, nseq, t, s0, mu, w0, a0, wda, g2, k_k, k_a, r_k, gn_g, gn_b, bd, c, dr):
    sw = pr.shape[1]
    npair = c // LANES
    nc = t // CHUNK
    g = _pick(npair, (8, 4, 2, 1))
    gw = g * LANES
    lw_ = sw - 3 * c
    assert (3 * c) % lw_ == 0 and c % gw == 0
    hb = CHUNK // SUBLANES
    lv, m1, tri = _chunk_masks()
    row_blk = lambda si, ci: si * nc + ci
    halo_blk = lambda si, ci: jnp.maximum((si * nc + ci) * hb - 1, 0)
    wide = lambda part: pl.BlockSpec((CHUNK, gw), lambda si, pi, ci: (row_blk(si, ci), part * (c // gw) + pi))
    wide_h = lambda part: pl.BlockSpec((SUBLANES, gw),
                                       lambda si, pi, ci: (halo_blk(si, ci), part * (c // gw) + pi))
    low = pl.BlockSpec((CHUNK, lw_), lambda si, pi, ci: (row_blk(si, ci), 3 * c // lw_))
    low_h = pl.BlockSpec((SUBLANES, lw_), lambda si, pi, ci: (halo_blk(si, ci), 3 * c // lw_))
    vec = lambda part: pl.BlockSpec((1, gw), lambda si, pi, ci: (0, part * (c // gw) + pi))
    vec_low = pl.BlockSpec((1, lw_), lambda si, pi, ci: (0, 3 * c // lw_))
    cols = lambda rows, part: pl.BlockSpec((rows, gw), lambda si, pi, ci: (0, part * (c // gw) + pi))
    st = pl.BlockSpec((1, g, HEAD, LANES), lambda si, pi, ci: (si, pi, 0, 0))
    const = lambda x: pl.BlockSpec(x.shape, lambda si, pi, ci: (0,) * x.ndim)
    return pl.pallas_call(
        functools.partial(_rwkv_chunk_kernel, g=g, dr=dr),
        grid=(nseq, npair // g, nc),
        in_specs=[wide(0), wide(1), wide(2), low, wide_h(0), wide_h(1), wide_h(2), low_h,
                  vec(0), vec(1), vec(2), vec_low,
                  vec(0), vec(0), cols(2 * dr, 0), cols(2 * dr, 1), cols(g2.shape[0], 0),
                  vec(0), vec(0), vec(0), vec(0), vec(0),
                  st, const(lv), const(m1), const(tri), const(bd)],
        out_specs=[pl.BlockSpec((CHUNK, gw), lambda si, pi, ci: (row_blk(si, ci), pi)), st],
        out_shape=[jax.ShapeDtypeStruct((nseq * t, c), BF16),
                   jax.ShapeDtypeStruct((nseq, npair, HEAD, LANES), F32)],
        compiler_params=_params(("parallel", "parallel", "arbitrary")),
        name="rwkv_chunk",
    )(pr, pr, pr, pr, pr, pr, pr, pr, mu, mu, mu, mu, w0, a0, wda, wda, g2, k_k, k_a, r_k, gn_g, gn_b,
      s0, lv, m1, tri, bd)


def _rwkv_post_kernel(y_ref, bn_ref, v_ref, g_ref, gg_ref, gb_ref, bd_ref, o_ref):
    o_ref[...] = _post_math(y_ref[...], bn_ref[...], v_ref[...], g_ref[...], gg_ref[...], gb_ref[...],
                            bd_ref[...]).astype(o_ref.dtype)


def _rwkv_post(y, bn, v, g, gn_g, gn_b, bd):
    m, c = y.shape
    tm = _pick(m, (256, 128, 64, 32, 16, 8))
    row = pl.BlockSpec((tm, c), lambda i: (i, 0))
    vec = pl.BlockSpec((1, c), lambda i: (0, 0))
    return pl.pallas_call(
        _rwkv_post_kernel,
        grid=(m // tm,),
        in_specs=[row, row, row, row, vec, vec, pl.BlockSpec((LANES, LANES), lambda i: (0, 0))],
        out_specs=row,
        out_shape=jax.ShapeDtypeStruct((m, c), BF16),
        compiler_params=_params(("parallel",)),
        name="rwkv_post",
    )(y, bn, v, g, gn_g, gn_b, bd)


def _unpack_state(s):
    n, hp = s.shape[0], s.shape[1]
    return s.reshape(n, hp, HEAD, 2, HEAD).transpose(0, 1, 3, 2, 4).reshape(n, 2 * hp, HEAD, HEAD)


def _layer(x_prompt, x_sample, wkv0, conv0, shift0,
           ln_mix_pre, ln_mix_post, ln_ffn_pre, ln_ffn_post, w_in, b_gate,
           conv_w, conv_b, conv_ln_g, conv_ln_b, w_conv_out, shift_mu,
           w0, w2, a0, a2, g2, k_k, k_a, r_k, gn_g, gn_b, w_rwkv_out, w_o,
           w_ffn_gate, w_ffn_up, w_ffn_down):
    bp, tp, d = x_prompt.shape
    bs, ts, _ = x_sample.shape
    c = conv_w.shape[1]
    kw = conv_w.shape[0]
    dr = w2.shape[0]
    sw = shift_mu.shape[0]
    mp, ms = bp * tp, bs * ts
    assert tp % CHUNK == 0 and tp >= kw - 1, "prompt sequences are processed in 64-token chunks"
    tm = _pick(math.gcd(mp, ms), (512, 256, 128, 64, 32, 16))
    row = lambda x: x.reshape(1, -1)

    w_g0 = w_in[:, 2 * c + sw:2 * c + sw + d].astype(BF16)
    w_g1 = w_in[:, 2 * c + sw + d:].astype(BF16)
    zeros = jnp.zeros((dr, c), F32)
    wda = jnp.concatenate([jnp.concatenate([w2, zeros], axis=1),
                           jnp.concatenate([zeros, a2], axis=1)], axis=0).astype(BF16)
    idx = jnp.arange(LANES) // HEAD
    bd = (idx[:, None] == idx[None, :]).astype(BF16)

    x_p = x_prompt.reshape(mp, d)
    x_s = x_sample.reshape(ms, d)
    h = _rms_cast(x_p, x_s, row(ln_mix_pre), min(tm, 256))

    conv_args = (conv_w, row(conv_b), row(conv_ln_g), row(conv_ln_b))
    u = _mm_pair(_mm_glu_kernel, h, w_in, 0, w_in, c, c, F32, "mm_glu")
    u_s = u[mp:].reshape(bs, ts, c)
    zc_p = _conv_seq(u, bp, tp, *conv_args)
    ext_s = jnp.concatenate([conv0, u_s], axis=1)
    zc_s = _conv_step(ext_s.transpose(1, 0, 2), *conv_args).transpose(1, 0, 2).reshape(ms, c)
    conv_p = jnp.stack([u[(q + 1) * tp - (kw - 1):(q + 1) * tp] for q in range(bp)])
    conv_s = ext_s[:, ts:]

    pr = _mm(h, w_in, sw, 2 * c, name="mm_shift")
    pr_s = pr[mp:].reshape(bs, ts, sw)
    shifted_s = jnp.concatenate([shift0[:, None], pr_s[:, :-1]], axis=1).reshape(ms, sw)
    g2_bf = g2.astype(BF16)
    consts = (row(shift_mu), row(w0), row(a0), wda, g2_bf, row(k_k), row(k_a), row(r_k), bd)
    yr_p, st_p = _rwkv_chunk(pr, bp, tp, jnp.zeros((bp, c // LANES, HEAD, LANES), F32), row(shift_mu),
                             row(w0), row(a0), wda, g2_bf, row(k_k), row(k_a), row(r_k),
                             row(gn_g), row(gn_b), bd, c, dr)
    vec_s = _rwkv_prep(pr_s.reshape(ms, sw), shifted_s, consts, c, dr)
    y_s, st_s = _wkv_step(*vec_s[:6], wkv0, bd, ts)
    yr_s = _rwkv_post(y_s, vec_s[7], vec_s[3], vec_s[6], row(gn_g), row(gn_b), bd)
    shift_p = pr[tp - 1:mp:tp]
    shift_s = pr_s[:, -1]

    mixed = _merge(h, zc_p, zc_s, yr_p, yr_s, w_g0, w_g1, w_conv_out.astype(BF16),
                   w_rwkv_out.astype(BF16), b_gate[0:1], b_gate[1:2], tm)
    o = _mm(mixed, w_o, name="mm_o")
    x1, h2 = _rms_res(x_p, x_s, o, row(ln_mix_post), row(ln_ffn_pre), min(tm, 256))

    act = _mm_pair(_mm_swiglu_kernel, h2, w_ffn_gate, 0, w_ffn_up, 0, w_ffn_gate.shape[1], BF16, "mm_swiglu")
    f = _mm(act, w_ffn_down.astype(BF16), name="mm_down")
    tl = min(tm, 256)
    y_prompt = _rms_res_last(x1, f, row(ln_ffn_post), 0, mp, tl)
    y_sample = _rms_res_last(x1, f, row(ln_ffn_post), mp, ms, tl)

    return (y_prompt.reshape(bp, tp, d), y_sample.reshape(bs, ts, d),
            _unpack_state(st_p), conv_p, shift_p,
            st_s, conv_s, shift_s)


def kernel(x_prompt, x_sample, state_wkv, state_conv, state_shift, ln_mix_pre, ln_mix_post, ln_ffn_pre,
           ln_ffn_post, w_in, b_gate, conv_w, conv_b, conv_ln_g, conv_ln_b, w_conv_out, shift_mu, w0, w2,
           a0, a2, g2, k_k, k_a, r_k, gn_g, gn_b, w_rwkv_out, w_o, w_ffn_gate, w_ffn_up, w_ffn_down):
    depth = w_in.shape[0]
    assert depth == 1, "one decoder layer per step"
    weights = (ln_mix_pre, ln_mix_post, ln_ffn_pre, ln_ffn_post, w_in, b_gate,
               conv_w, conv_b, conv_ln_g, conv_ln_b, w_conv_out, shift_mu,
               w0, w2, a0, a2, g2, k_k, k_a, r_k, gn_g, gn_b, w_rwkv_out, w_o,
               w_ffn_gate, w_ffn_up, w_ffn_down)
    lw = tuple(wt[0] for wt in weights)
    yp, ys, wkv_p, conv_p, shift_p, wkv_s, conv_s, shift_s = _layer(
        x_prompt, x_sample, state_wkv[0], state_conv[0], state_shift[0], *lw)
    return (yp, ys, wkv_p[None], conv_p[None], shift_p[None],
            wkv_s[None], conv_s[None], shift_s[None])
```

```python
import functools
import math

import jax
import jax.numpy as jnp
from jax import lax
from jax.experimental import pallas as pl
from jax.experimental.pallas import tpu as pltpu

F32 = jnp.float32
BF16 = jnp.bfloat16

RMS_EPS = 1e-6
LN_EPS = 1e-5
GN_EPS = 64e-5
HEAD = 64
LANES = 128
SUBLANES = 8
CONV_HALO = 32
CHUNK = 64
SPLIT_FROM_LEVEL = 3
VMEM_LIMIT = 56 * 1024 * 1024


def _pick(n, cands):
    for c in cands:
        if n % c == 0:
            return c
    return n


def _params(sem):
    return pltpu.CompilerParams(dimension_semantics=sem, vmem_limit_bytes=VMEM_LIMIT)


def _dot(a, b):
    return jnp.dot(a, b, preferred_element_type=F32)


def _split(x):
    hi = x.astype(BF16)
    return hi, (x - hi.astype(F32)).astype(BF16)


def _mm_split_lhs(a, b):
    hi, lo = _split(a)
    bb = b.astype(BF16)
    return _dot(hi, bb) + _dot(lo, bb)


def _mm_split_both(a, b):
    ah, al = _split(a)
    bh, bl = _split(b)
    return _dot(ah, bh) + _dot(al, bh) + _dot(ah, bl)


def _head_sum(x, bd):
    cols = [_dot(x[:, c:c + LANES].astype(BF16), bd) for c in range(0, x.shape[1], LANES)]
    return jnp.concatenate(cols, axis=1)


def _two_group_specs(tm, d, nblk_p):
    return (pl.BlockSpec((tm, d), lambda i, *_: (jnp.minimum(i, nblk_p - 1), 0)),
            pl.BlockSpec((tm, d), lambda i, *_: (jnp.maximum(i - nblk_p, 0), 0)))


def _two_group_rows(p_ref, s_ref, nblk_p):
    return jnp.where(pl.program_id(0) < nblk_p, p_ref[...], s_ref[...])


def _rms_cast_kernel(xp_ref, xs_ref, g_ref, o_ref, *, nblk_p):
    x = _two_group_rows(xp_ref, xs_ref, nblk_p)
    ms = jnp.mean(x * x, axis=-1, keepdims=True)
    o_ref[...] = (x * lax.rsqrt(ms + RMS_EPS) * g_ref[...]).astype(o_ref.dtype)


def _rms_cast(x_p, x_s, g, tm):
    (mp, d), ms = x_p.shape, x_s.shape[0]
    xp_spec, xs_spec = _two_group_specs(tm, d, mp // tm)
    return pl.pallas_call(
        functools.partial(_rms_cast_kernel, nblk_p=mp // tm),
        grid=((mp + ms) // tm,),
        in_specs=[xp_spec, xs_spec, pl.BlockSpec((1, d), lambda i: (0, 0))],
        out_specs=pl.BlockSpec((tm, d), lambda i: (i, 0)),
        out_shape=jax.ShapeDtypeStruct((mp + ms, d), BF16),
        compiler_params=_params(("parallel",)),
        name="rms_cast",
    )(x_p, x_s, g)


def _rms_res_kernel(xp_ref, xs_ref, o_ref, g_ref, g2_ref, x1_ref, h_ref, *, nblk_p):
    o = o_ref[...]
    ms = jnp.mean(o * o, axis=-1, keepdims=True)
    x1 = _two_group_rows(xp_ref, xs_ref, nblk_p) + o * lax.rsqrt(ms + RMS_EPS) * g_ref[...]
    x1_ref[...] = x1
    ms1 = jnp.mean(x1 * x1, axis=-1, keepdims=True)
    h_ref[...] = (x1 * lax.rsqrt(ms1 + RMS_EPS) * g2_ref[...]).astype(h_ref.dtype)


def _rms_res(x_p, x_s, o, g, g2, tm):
    (mp, d), m = x_p.shape, o.shape[0]
    xp_spec, xs_spec = _two_group_specs(tm, d, mp // tm)
    row = pl.BlockSpec((tm, d), lambda i: (i, 0))
    vec = pl.BlockSpec((1, d), lambda i: (0, 0))
    return pl.pallas_call(
        functools.partial(_rms_res_kernel, nblk_p=mp // tm),
        grid=(m // tm,),
        in_specs=[xp_spec, xs_spec, row, vec, vec],
        out_specs=[row, row],
        out_shape=[jax.ShapeDtypeStruct((m, d), F32), jax.ShapeDtypeStruct((m, d), BF16)],
        compiler_params=_params(("parallel",)),
        name="rms_res",
    )(x_p, x_s, o, g, g2)


def _rms_res_last_kernel(x_ref, o_ref, g_ref, x1_ref):
    o = o_ref[...]
    ms = jnp.mean(o * o, axis=-1, keepdims=True)
    x1_ref[...] = x_ref[...] + o * lax.rsqrt(ms + RMS_EPS) * g_ref[...]


def _rms_res_last(x, o, g, row0, rows, tm):
    d = x.shape[1]
    blk0 = row0 // tm
    src = pl.BlockSpec((tm, d), lambda i: (blk0 + i, 0))
    return pl.pallas_call(
        _rms_res_last_kernel,
        grid=(rows // tm,),
        in_specs=[src, src, pl.BlockSpec((1, d), lambda i: (0, 0))],
        out_specs=pl.BlockSpec((tm, d), lambda i: (i, 0)),
        out_shape=jax.ShapeDtypeStruct((rows, d), F32),
        compiler_params=_params(("parallel",)),
        name="rms_res_last",
    )(x, o, g)


MM_VMEM_BUDGET = 50 * 1024 * 1024


def _mm_tiles(m, k, n, col0s, w_bytes, n_w, out_bytes):
    span = math.gcd(n, *col0s)
    tns = [t for t in (512, 256, 128) if span % t == 0] or [span]
    for tm in (1088, 1024, 544, 512, 256, 128, 64, 32, 16, 8):
        for tn in tns:
            need = (2 * tm * k * 2
                    + n_w * k * tn * (2 * w_bytes + (2 if w_bytes > 2 else 0))
                    + tm * tn * (2 * out_bytes + 4 * n_w))
            if m % tm == 0 and need <= MM_VMEM_BUDGET:
                return tm, tn
    return m, tns[-1]


def _mm_kernel(x_ref, w_ref, o_ref):
    o_ref[...] = _dot(x_ref[...], w_ref[...].astype(BF16)).astype(o_ref.dtype)


def _mm(x, w, n=None, col0=0, out_dtype=F32, name="mm"):
    m, k = x.shape
    n = w.shape[1] if n is None else n
    tm, tn = _mm_tiles(m, k, n, (col0,), w.dtype.itemsize, 1, jnp.dtype(out_dtype).itemsize)
    j0 = col0 // tn
    return pl.pallas_call(
        _mm_kernel,
        grid=(m // tm, n // tn),
        in_specs=[pl.BlockSpec((tm, k), lambda i, j: (i, 0)),
                  pl.BlockSpec((k, tn), lambda i, j: (0, j0 + j))],
        out_specs=pl.BlockSpec((tm, tn), lambda i, j: (i, j)),
        out_shape=jax.ShapeDtypeStruct((m, n), out_dtype),
        compiler_params=_params(("parallel", "arbitrary")),
        name=name,
    )(x, w)


def _mm_glu_kernel(x_ref, wa_ref, wb_ref, o_ref):
    x = x_ref[...]
    o_ref[...] = _dot(x, wa_ref[...].astype(BF16)) * jax.nn.sigmoid(_dot(x, wb_ref[...].astype(BF16)))


def _mm_swiglu_kernel(x_ref, wg_ref, wu_ref, o_ref):
    x = x_ref[...]
    o_ref[...] = (jax.nn.silu(_dot(x, wg_ref[...].astype(BF16)))
                  * _dot(x, wu_ref[...].astype(BF16))).astype(o_ref.dtype)


def _mm_pair(kern, x, wa, col_a, wb, col_b, n, out_dtype, name):
    m, k = x.shape
    tm, tn = _mm_tiles(m, k, n, (col_a, col_b), wa.dtype.itemsize, 2, jnp.dtype(out_dtype).itemsize)
    ja, jb = col_a // tn, col_b // tn
    return pl.pallas_call(
        kern,
        grid=(m // tm, n // tn),
        in_specs=[pl.BlockSpec((tm, k), lambda i, j: (i, 0)),
                  pl.BlockSpec((k, tn), lambda i, j: (0, ja + j)),
                  pl.BlockSpec((k, tn), lambda i, j: (0, jb + j))],
        out_specs=pl.BlockSpec((tm, tn), lambda i, j: (i, j)),
        out_shape=jax.ShapeDtypeStruct((m, n), out_dtype),
        compiler_params=_params(("parallel", "arbitrary")),
        name=name,
    )(x, wa, wb)


def _merge_kernel(h_ref, zcp_ref, zcs_ref, yrp_ref, yrs_ref, wg0_ref, wg1_ref, wc_ref, wr_ref,
                  b0_ref, b1_ref, o_ref, *, nblk_p):
    h = h_ref[...]
    g0 = jax.nn.sigmoid(_dot(h, wg0_ref[...]) + b0_ref[...])
    g1 = jax.nn.sigmoid(_dot(h, wg1_ref[...]) + b1_ref[...])
    oc = _dot(_two_group_rows(zcp_ref, zcs_ref, nblk_p), wc_ref[...])
    orr = _dot(_two_group_rows(yrp_ref, yrs_ref, nblk_p), wr_ref[...])
    o_ref[...] = (g0 * oc + g1 * orr).astype(o_ref.dtype)


def _cast_kernel(w_ref, o_ref):
    o_ref[...] = w_ref[...].astype(o_ref.dtype)


def _cast_cols(w, col0, n):
    k = w.shape[0]
    tn = _pick(math.gcd(n, col0), (256, 128))
    j0 = col0 // tn
    return pl.pallas_call(
        _cast_kernel,
        grid=(n // tn,),
        in_specs=[pl.BlockSpec((k, tn), lambda j: (0, j0 + j))],
        out_specs=pl.BlockSpec((k, tn), lambda j: (0, j)),
        out_shape=jax.ShapeDtypeStruct((k, n), BF16),
        compiler_params=_params(("parallel",)),
        name="cast_cols",
    )(w)


def _merge(h, zc_p, zc_s, yr_p, yr_s, wg, wc, wr, b0, b1, tm):
    m, d = h.shape
    mp, c = zc_p.shape
    tn = _pick(d, (256, 128))
    gp_spec, gs_spec = _two_group_specs(tm, c, mp // tm)
    return pl.pallas_call(
        functools.partial(_merge_kernel, nblk_p=mp // tm),
        grid=(m // tm, d // tn),
        in_specs=[pl.BlockSpec((tm, d), lambda i, j: (i, 0)),
                  gp_spec, gs_spec, gp_spec, gs_spec,
                  pl.BlockSpec((d, tn), lambda i, j: (0, j)),
                  pl.BlockSpec((d, tn), lambda i, j: (0, d // tn + j)),
                  pl.BlockSpec((c, tn), lambda i, j: (0, j)),
                  pl.BlockSpec((c, tn), lambda i, j: (0, j)),
                  pl.BlockSpec((1, tn), lambda i, j: (0, j)),
                  pl.BlockSpec((1, tn), lambda i, j: (0, j))],
        out_specs=pl.BlockSpec((tm, tn), lambda i, j: (i, j)),
        out_shape=jax.ShapeDtypeStruct((m, d), BF16),
        compiler_params=_params(("parallel", "arbitrary")),
        name="merge",
    )(h, zc_p, zc_s, yr_p, yr_s, wg, wg, wc, wr, b0, b1)


def _ln_silu(z, g, b):
    mu = jnp.mean(z, axis=-1, keepdims=True)
    zc = z - mu
    var = jnp.mean(zc * zc, axis=-1, keepdims=True)
    y = zc * lax.rsqrt(var + LN_EPS) * g + b
    return y * jax.nn.sigmoid(y)


def _conv_seq_kernel(um_ref, uh_ref, w_ref, cb_ref, g_ref, b_ref, o_ref, x_scr, xs_scr, z_scr,
                     *, tb, kw, lc):
    i = pl.program_id(1)
    c = um_ref.shape[1]
    halo = uh_ref[...]
    x_scr[0:CONV_HALO, :] = jnp.where(i == 0, jnp.zeros_like(halo), halo)
    x_scr[CONV_HALO:CONV_HALO + tb, :] = um_ref[...]
    off = CONV_HALO - (kw - 1)
    first = True
    for s in range(SUBLANES):
        taps = [j for j in range(kw) if (off + j) % SUBLANES == s]
        if not taps:
            continue
        span = max(off + j - s for j in taps) + tb
        xs_scr[0:span, :] = x_scr[s:s + span, :]
        for c0 in range(0, c, lc):
            cols = slice(c0, c0 + lc)
            acc = cb_ref[:, cols] if first else z_scr[:, cols]
            for j in taps:
                a0 = off + j - s
                acc = acc + xs_scr[a0:a0 + tb, cols] * w_ref[j:j + 1, cols]
            z_scr[:, cols] = acc
        first = False
    o_ref[...] = _ln_silu(z_scr[...], g_ref[...], b_ref[...]).astype(o_ref.dtype)


def _conv_seq(u, nseq, t, conv_w, conv_b, ln_g, ln_b):
    c = u.shape[1]
    kw = conv_w.shape[0]
    tb = _pick(t, (64, 32))
    lc = _pick(c, (256, 128))
    nb = t // tb
    hb = tb // CONV_HALO
    vec = pl.BlockSpec((1, c), lambda bi, i: (0, 0))
    return pl.pallas_call(
        functools.partial(_conv_seq_kernel, tb=tb, kw=kw, lc=lc),
        grid=(nseq, nb),
        in_specs=[pl.BlockSpec((tb, c), lambda bi, i: (bi * nb + i, 0)),
                  pl.BlockSpec((CONV_HALO, c), lambda bi, i: (jnp.maximum((bi * nb + i) * hb - 1, 0), 0)),
                  pl.BlockSpec((kw, c), lambda bi, i: (0, 0)),
                  vec, vec, vec],
        out_specs=pl.BlockSpec((tb, c), lambda bi, i: (bi * nb + i, 0)),
        out_shape=jax.ShapeDtypeStruct((nseq * t, c), BF16),
        scratch_shapes=[pltpu.VMEM((CONV_HALO + tb, c), F32), pltpu.VMEM((CONV_HALO + tb, c), F32),
                        pltpu.VMEM((tb, c), F32)],
        compiler_params=_params(("parallel", "arbitrary")),
        name="conv_seq",
    )(u, u, conv_w, conv_b, ln_g, ln_b)


def _conv_step_kernel(e_ref, w_ref, cb_ref, g_ref, b_ref, o_ref, *, kw, nt):
    for t in range(nt):
        acc = e_ref[t] * w_ref[0:1, :]
        for j in range(1, kw):
            acc = acc + e_ref[t + j] * w_ref[j:j + 1, :]
        o_ref[t] = _ln_silu(acc + cb_ref[...], g_ref[...], b_ref[...]).astype(o_ref.dtype)


def _conv_step(ext_tm, conv_w, conv_b, ln_g, ln_b):
    te, b, c = ext_tm.shape
    kw = conv_w.shape[0]
    nt = te - (kw - 1)
    sb = _pick(b, (8,))
    vec = pl.BlockSpec((1, c), lambda i: (0, 0))
    return pl.pallas_call(
        functools.partial(_conv_step_kernel, kw=kw, nt=nt),
        grid=(b // sb,),
        in_specs=[pl.BlockSpec((te, sb, c), lambda i: (0, i, 0)),
                  pl.BlockSpec((kw, c), lambda i: (0, 0)),
                  vec, vec, vec],
        out_specs=pl.BlockSpec((nt, sb, c), lambda i: (0, i, 0)),
        out_shape=jax.ShapeDtypeStruct((nt, b, c), BF16),
        compiler_params=_params(("parallel",)),
        name="conv_step",
    )(ext_tm, conv_w, conv_b, ln_g, ln_b)


def _rwkv_vector_math(m_r, m_k, m_v, m_low, w0, a0, wd_w, wd_a, g2, k_k, k_a, r_k, bd, dr):
    low = m_low[:, 0:2 * dr]
    lane = lax.broadcasted_iota(jnp.int32, low.shape, 1)
    low = jnp.where(lane < dr, jnp.tanh(low), low).astype(BF16)
    logw = -math.exp(-0.5) * jax.nn.sigmoid(w0 + _dot(low, wd_w))
    rate = jax.nn.sigmoid(a0 + _dot(low, wd_a))
    gate = _dot(jax.nn.sigmoid(m_low[:, 2 * dr:]).astype(BF16), g2)
    kk = m_k * k_k
    kk = kk / jnp.maximum(jnp.sqrt(_head_sum(kk * kk, bd)), 1e-12)
    k2 = m_k * (1.0 + (rate - 1.0) * k_a)
    bonus = _head_sum(m_r * k2 * r_k, bd)
    return m_r, logw, k2, m_v, -kk, kk * rate, gate, bonus


def _rwkv_prep_kernel(pr_ref, sh_ref, mu_ref, w0_ref, a0_ref, wda_ref, g2_ref, kk_ref, ka_ref, rk_ref,
                      bd_ref, *outs, c, dr):
    pr = pr_ref[...]
    m = pr + (sh_ref[...] - pr) * mu_ref[...]
    vals = _rwkv_vector_math(m[:, 0:c], m[:, c:2 * c], m[:, 2 * c:3 * c], m[:, 3 * c:],
                             w0_ref[...], a0_ref[...], wda_ref[:, 0:c], wda_ref[:, c:2 * c], g2_ref[...],
                             kk_ref[...], ka_ref[...], rk_ref[...], bd_ref[...], dr)
    for o_ref, val in zip(outs, vals):
        o_ref[...] = val


def _rwkv_prep(pr, shifted, consts, c, dr):
    rows, sw = pr.shape
    tm = _pick(rows, (128, 64, 32, 16, 8))
    row = pl.BlockSpec((tm, sw), lambda i: (i, 0))
    out = pl.BlockSpec((tm, c), lambda i: (i, 0))
    full = lambda x: pl.BlockSpec(x.shape, lambda i: (0,) * x.ndim)
    return pl.pallas_call(
        functools.partial(_rwkv_prep_kernel, c=c, dr=dr),
        grid=(rows // tm,),
        in_specs=[row, row] + [full(x) for x in consts],
        out_specs=[out] * 8,
        out_shape=[jax.ShapeDtypeStruct((rows, c), F32)] * 8,
        compiler_params=_params(("parallel",)),
        name="rwkv_prep",
    )(pr, shifted, *consts)


def _wkv_step_kernel(r_ref, w_ref, k_ref, v_ref, a_ref, b_ref, s0_ref, bd_ref, y_ref, st_ref,
                     *, nseq, t, g):
    bd = bd_ref[...]
    rows = lax.broadcasted_iota(jnp.int32, (HEAD, LANES), 0)
    lanes = lax.broadcasted_iota(jnp.int32, (HEAD, LANES), 1)
    diag = jnp.where(rows == lanes % HEAD, 1.0, 0.0).astype(F32)
    chains = [(q, p) for q in range(nseq) for p in range(g)]
    rr, kk, vv, aa, bb = (ref[...] for ref in (r_ref, k_ref, v_ref, a_ref, b_ref))
    ww = jnp.exp(w_ref[...])
    s = [jnp.concatenate([s0_ref[q, 2 * p], s0_ref[q, 2 * p + 1]], axis=1) for q, p in chains]
    yrows = {}
    for i in range(t):
        bc = lambda x, q, p: jnp.broadcast_to(
            x[q * t + i:q * t + i + 1, p * LANES:(p + 1) * LANES], (HEAD, LANES))
        lhs = jnp.concatenate(
            [jnp.concatenate([s[n] * bc(aa, q, p), bc(vv, q, p) * diag], axis=0)
             for n, (q, p) in enumerate(chains)], axis=0)
        res = _dot(lhs.astype(BF16), bd)
        s = [s[n] * bc(ww, q, p) + res[2 * n * HEAD:(2 * n + 1) * HEAD] * bc(bb, q, p)
             + res[(2 * n + 1) * HEAD:(2 * n + 2) * HEAD] * bc(kk, q, p)
             for n, (q, p) in enumerate(chains)]
        sr = jnp.concatenate([s[n] * bc(rr, q, p) for n, (q, p) in enumerate(chains)], axis=0)
        yc = _dot(sr.astype(BF16), bd)
        for n, (q, p) in enumerate(chains):
            yrows[(q, i, p)] = jnp.sum(yc[n * HEAD:(n + 1) * HEAD] * diag, axis=0, keepdims=True)
    y_ref[...] = jnp.concatenate(
        [jnp.concatenate([yrows[(q, i, p)] for p in range(g)], axis=1)
         for q in range(nseq) for i in range(t)], axis=0)
    for n, (q, p) in enumerate(chains):
        st_ref[q, 2 * p] = s[n][:, 0:HEAD]
        st_ref[q, 2 * p + 1] = s[n][:, HEAD:2 * HEAD]


def _wkv_step(r, lw, k, v, a, b, s0, bd, t):
    rows, c = r.shape
    npair = c // LANES
    assert SUBLANES % t == 0, "token-by-token path expects a few new tokens per sequence"
    nseq = SUBLANES // t
    g = _pick(npair, (8, 4, 2, 1))
    seq = pl.BlockSpec((nseq * t, g * LANES), lambda si, pi: (si, pi))
    st = pl.BlockSpec((nseq, 2 * g, HEAD, HEAD), lambda si, pi: (si, pi, 0, 0))
    return pl.pallas_call(
        functools.partial(_wkv_step_kernel, nseq=nseq, t=t, g=g),
        grid=(rows // (nseq * t), npair // g),
        in_specs=[seq] * 6 + [st, pl.BlockSpec((LANES, LANES), lambda si, pi: (0, 0))],
        out_specs=[seq, st],
        out_shape=[jax.ShapeDtypeStruct((rows, c), F32),
                   jax.ShapeDtypeStruct(s0.shape, F32)],
        compiler_params=_params(("parallel", "parallel")),
        name="wkv_step",
    )(r, lw, k, v, a, b, s0, bd)


def _wkv_lanes_kernel(r_ref, w_ref, k_ref, v_ref, a_ref, b_ref, s0_ref, y_ref, st_ref, xt_scr, yt_scr,
                      *, t, nb):
    names = (r_ref, w_ref, k_ref, v_ref, a_ref, b_ref)
    for i, ref in enumerate(names):
        for step in range(t):
            x = ref[step * nb:(step + 1) * nb, :].T
            xt_scr[i, step] = jnp.exp(x) if ref is w_ref else x
    ir, iw, ik, iv, ia, ib = range(6)
    sub = lax.broadcasted_iota(jnp.int32, (SUBLANES, nb), 0)
    for hh in range(2):
        ch = slice(hh * HEAD, (hh + 1) * HEAD)

        def group(vg, carry, hh=hh, ch=ch):
            v0 = pl.multiple_of(vg * SUBLANES, SUBLANES)
            vt = [xt_scr[iv, step, pl.ds(hh * HEAD + v0, SUBLANES), :] for step in range(t)]
            ytile = [jnp.zeros((SUBLANES, nb), F32) for _ in range(t)]
            for vi in range(SUBLANES):
                s = s0_ref[hh, v0 + vi]
                for step in range(t):
                    sa = jnp.sum(s * xt_scr[ia, step, ch, :], axis=0, keepdims=True)
                    s = (s * xt_scr[iw, step, ch, :] + sa * xt_scr[ib, step, ch, :]
                         + vt[step][vi:vi + 1, :] * xt_scr[ik, step, ch, :])
                    yrow = jnp.sum(s * xt_scr[ir, step, ch, :], axis=0, keepdims=True)
                    ytile[step] = jnp.where(sub == vi, yrow, ytile[step])
                st_ref[hh, v0 + vi] = s
            for step in range(t):
                yt_scr[step, pl.ds(hh * HEAD + v0, SUBLANES), :] = ytile[step]
            return carry

        lax.fori_loop(0, HEAD // SUBLANES, group, 0)
    for step in range(t):
        y_ref[step * nb:(step + 1) * nb, :] = yt_scr[step].T


def _wkv_lanes(r, lw, k, v, a, b, s0_t, t):
    rows, c = r.shape
    nb = rows // t
    seq = pl.BlockSpec((rows, LANES), lambda p: (0, p))
    st = pl.BlockSpec((2, HEAD, HEAD, nb), lambda p: (p, 0, 0, 0))
    return pl.pallas_call(
        functools.partial(_wkv_lanes_kernel, t=t, nb=nb),
        grid=(c // LANES,),
        in_specs=[seq] * 6 + [st],
        out_specs=[seq, st],
        out_shape=[jax.ShapeDtypeStruct((rows, c), F32), jax.ShapeDtypeStruct(s0_t.shape, F32)],
        scratch_shapes=[pltpu.VMEM((6, t, LANES, nb), F32), pltpu.VMEM((t, LANES, nb), F32)],
        compiler_params=_params(("parallel",)),
        name="wkv_lanes",
    )(r, lw, k, v, a, b, s0_t)


def _chunk_masks():
    n = 2 * CHUNK
    ri = jnp.arange(n)[:, None]
    ci = jnp.arange(n)[None, :]
    levels = [(ri // 2) == (ci // 2)]
    bsz = 2
    while bsz < CHUNK:
        levels.append(((ri // (2 * bsz)) == (ci // (2 * bsz))) & ((ri // bsz) != (ci // bsz)))
        bsz *= 2
    levels.append(ri == ci)
    lv = jnp.stack(levels).astype(F32)
    r4 = jnp.arange(2 * n)[:, None]
    c4 = jnp.arange(2 * n)[None, :]
    same_head = ((r4 // CHUNK) % 2) == ((c4 // CHUNK) % 2)
    t, s = r4 % CHUNK, c4 % CHUNK
    m1 = jnp.where(r4 < n, same_head & (s < t), same_head & (s <= t)).astype(F32)
    tri = (jnp.arange(CHUNK)[:, None] >= jnp.arange(CHUNK)[None, :]).astype(BF16)
    return lv, m1, tri


def _post_math(y, bonus, v, gate, gn_g, gn_b, bd):
    ym = _head_sum(y, bd) * (1.0 / HEAD)
    yc = y - ym
    yv = _head_sum(yc * yc, bd) * (1.0 / HEAD)
    yn = yc * lax.rsqrt(yv + GN_EPS) * gn_g + gn_b
    return (yn + bonus * v) * gate


def _rwkv_chunk_kernel(pr_r, pr_k, pr_v, pr_l, hr_ref, hk_ref, hv_ref, hl_ref, mu_r, mu_k, mu_v, mu_l,
                       w0_ref, a0_ref, wdw_ref, wda_ref, g2_ref, kk_ref, ka_ref, rk_ref, gg_ref, gb_ref,
                       s0_ref, lv_ref, m1_ref, tri_ref, bd_ref, y_ref, st_ref, *, g, dr):
    ci = pl.program_id(2)
    n = 2 * CHUNK
    lane = lax.broadcasted_iota(jnp.int32, (1, LANES), 1)
    m0 = jnp.where(lane < HEAD, 1.0, 0.0).astype(F32)
    m1 = 1.0 - m0
    expand = lambda x: jnp.concatenate([x * m0, x * m1], axis=0)
    fold = lambda x: x[0:CHUNK] + x[CHUNK:n]

    @pl.when(ci == 0)
    def _():
        st_ref[...] = s0_ref[...]

    def shifted_lerp(x_ref, h_ref, mu_ref):
        x = x_ref[...]
        prev = h_ref[SUBLANES - 1:SUBLANES, :]
        prev = jnp.where(ci == 0, jnp.zeros_like(prev), prev)
        row = lax.broadcasted_iota(jnp.int32, x.shape, 0)
        shifted = jnp.where(row == 0, jnp.broadcast_to(prev, x.shape), pltpu.roll(x, 1, 0))
        return x + (shifted - x) * mu_ref[...]

    bd = bd_ref[...]
    r_all, lw_all, k_all, v_all, a_all, b_all, gate, bonus = _rwkv_vector_math(
        shifted_lerp(pr_r, hr_ref, mu_r), shifted_lerp(pr_k, hk_ref, mu_k), shifted_lerp(pr_v, hv_ref, mu_v),
        shifted_lerp(pr_l, hl_ref, mu_l), w0_ref[...], a0_ref[...], wdw_ref[...], wda_ref[...], g2_ref[...],
        kk_ref[...], ka_ref[...], rk_ref[...], bd, dr)

    pairs = range(g)
    tri = tri_ref[...]
    mask1 = m1_ref[...]
    nlev = lv_ref.shape[0]
    lv = [lv_ref[i] for i in range(nlev)]
    sls = [slice(p * LANES, (p + 1) * LANES) for p in pairs]
    r, lw, k, v, a, b = ([x[:, sl] for sl in sls] for x in (r_all, lw_all, k_all, v_all, a_all, b_all))
    s_old = [expand(st_ref[0, p]) for p in pairs]
    lw_split = [_split(x) for x in lw]
    cum = [_dot(tri, hi) + _dot(tri, lo) for hi, lo in lw_split]
    cum_l = [x[CHUNK - 1:CHUNK] for x in cum]
    e_neg = [jnp.exp(-x) for x in cum]
    e_hat = [jnp.exp(cl - x) for cl, x in zip(cum_l, cum)]
    at_e = [expand(a[p] * jnp.exp(cum[p] - lw[p])) for p in pairs]
    rt_e = [expand(r[p] * jnp.exp(cum[p])) for p in pairs]
    v_e = [expand(x) for x in v]
    bt = [(b[p] * e_neg[p]).astype(BF16) for p in pairs]
    kt = [(k[p] * e_neg[p]).astype(BF16) for p in pairs]
    nt = (((1,), (1,)), ((), ()))
    out1 = [lax.dot_general(jnp.concatenate([at_e[p], rt_e[p]], axis=0).astype(BF16),
                            jnp.concatenate([bt[p], bt[p], kt[p], kt[p]], axis=0), nt,
                            preferred_element_type=F32) for p in pairs]
    out1 = [jnp.where(mask1 != 0.0, x, 0.0) for x in out1]
    a_ab = [x[0:n, 0:n] for x in out1]
    m_r = [x[n:2 * n, :].astype(BF16) for x in out1]
    akv = [_dot(out1[p][0:n, n:2 * n].astype(BF16), v_e[p].astype(BF16)) for p in pairs]
    tm = [lv[nlev - 1] + x * lv[0] for x in a_ab]
    for lev in range(1, nlev - 1):
        mm = _mm_split_lhs if lev > SPLIT_FROM_LEVEL else (lambda x, y: _dot(x.astype(BF16), y.astype(BF16)))
        step = [mm(tm[p], a_ab[p] * lv[lev]) for p in pairs]
        tm = [tm[p] + mm(step[p], tm[p]) for p in pairs]
    wu = [_mm_split_both(tm[p], jnp.concatenate([at_e[p], akv[p]], axis=1)) for p in pairs]
    rhs4 = [jnp.concatenate([wu[p], jnp.concatenate([jnp.zeros_like(v_e[p]), v_e[p]], axis=1)],
                            axis=0).astype(BF16) for p in pairs]
    o4 = [_dot(m_r[p], rhs4[p]) for p in pairs]
    q = [fold(rt_e[p] + o4[p][:, 0:LANES]).astype(BF16) for p in pairs]
    y1 = [fold(o4[p][:, LANES:2 * LANES]) for p in pairs]
    rhs5 = [jnp.concatenate([expand(b[p] * e_hat[p]), expand(k[p] * e_hat[p])], axis=0).astype(BF16)
            for p in pairs]
    o5 = [lax.dot_general(rhs4[p], rhs5[p], (((0,), (0,)), ((), ())), preferred_element_type=F32)
          for p in pairs]
    s_bf = [x.astype(BF16) for x in s_old]
    y = [lax.dot_general(q[p], s_bf[p], nt, preferred_element_type=F32) + y1[p] for p in pairs]
    s_new = [s_old[p] * jnp.exp(cum_l[p]) + _dot(s_bf[p], o5[p][0:n].astype(BF16)) + o5[p][n:2 * n]
             for p in pairs]
    y_all = jnp.concatenate(y, axis=1)
    y_ref[...] = _post_math(y_all, bonus, v_all, gate, gg_ref[...], gb_ref[...], bd).astype(y_ref.dtype)
    for p in pairs:
        st_ref[0, p] = fold(s_new[p])


def _rwkv_chunk(pr, nseq, t, s0, mu, w0, a0, wda, g2, k_k, k_a, r_k, gn_g, gn_b, bd, c, dr):
    sw = pr.shape[1]
    npair = c // LANES
    nc = t // CHUNK
    g = _pick(npair, (8, 4, 2, 1))
    gw = g * LANES
    lw_ = sw - 3 * c
    assert (3 * c) % lw_ == 0 and c % gw == 0
    hb = CHUNK // SUBLANES
    lv, m1, tri = _chunk_masks()
    row_blk = lambda si, ci: si * nc + ci
    halo_blk = lambda si, ci: jnp.maximum((si * nc + ci) * hb - 1, 0)
    wide = lambda part: pl.BlockSpec((CHUNK, gw), lambda si, pi, ci: (row_blk(si, ci), part * (c // gw) + pi))
    wide_h = lambda part: pl.BlockSpec((SUBLANES, gw),
                                       lambda si, pi, ci: (halo_blk(si, ci), part * (c // gw) + pi))
    low = pl.BlockSpec((CHUNK, lw_), lambda si, pi, ci: (row_blk(si, ci), 3 * c // lw_))
    low_h = pl.BlockSpec((SUBLANES, lw_), lambda si, pi, ci: (halo_blk(si, ci), 3 * c // lw_))
    vec = lambda part: pl.BlockSpec((1, gw), lambda si, pi, ci: (0, part * (c // gw) + pi))
    vec_low = pl.BlockSpec((1, lw_), lambda si, pi, ci: (0, 3 * c // lw_))
    cols = lambda rows, part: pl.BlockSpec((rows, gw), lambda si, pi, ci: (0, part * (c // gw) + pi))
    st = pl.BlockSpec((1, g, HEAD, LANES), lambda si, pi, ci: (si, pi, 0, 0))
    const = lambda x: pl.BlockSpec(x.shape, lambda si, pi, ci: (0,) * x.ndim)
    return pl.pallas_call(
        functools.partial(_rwkv_chunk_kernel, g=g, dr=dr),
        grid=(nseq, npair // g, nc),
        in_specs=[wide(0), wide(1), wide(2), low, wide_h(0), wide_h(1), wide_h(2), low_h,
                  vec(0), vec(1), vec(2), vec_low,
                  vec(0), vec(0), cols(2 * dr, 0), cols(2 * dr, 1), cols(g2.shape[0], 0),
                  vec(0), vec(0), vec(0), vec(0), vec(0),
                  st, const(lv), const(m1), const(tri), const(bd)],
        out_specs=[pl.BlockSpec((CHUNK, gw), lambda si, pi, ci: (row_blk(si, ci), pi)), st],
        out_shape=[jax.ShapeDtypeStruct((nseq * t, c), BF16),
                   jax.ShapeDtypeStruct((nseq, npair, HEAD, LANES), F32)],
        compiler_params=_params(("parallel", "parallel", "arbitrary")),
        name="rwkv_chunk",
    )(pr, pr, pr, pr, pr, pr, pr, pr, mu, mu, mu, mu, w0, a0, wda, wda, g2, k_k, k_a, r_k, gn_g, gn_b,
      s0, lv, m1, tri, bd)


def _rwkv_post_kernel(y_ref, bn_ref, v_ref, g_ref, gg_ref, gb_ref, bd_ref, o_ref):
    o_ref[...] = _post_math(y_ref[...], bn_ref[...], v_ref[...], g_ref[...], gg_ref[...], gb_ref[...],
                            bd_ref[...]).astype(o_ref.dtype)


def _rwkv_post(y, bn, v, g, gn_g, gn_b, bd):
    m, c = y.shape
    tm = _pick(m, (256, 128, 64, 32, 16, 8))
    row = pl.BlockSpec((tm, c), lambda i: (i, 0))
    vec = pl.BlockSpec((1, c), lambda i: (0, 0))
    return pl.pallas_call(
        _rwkv_post_kernel,
        grid=(m // tm,),
        in_specs=[row, row, row, row, vec, vec, pl.BlockSpec((LANES, LANES), lambda i: (0, 0))],
        out_specs=row,
        out_shape=jax.ShapeDtypeStruct((m, c), BF16),
        compiler_params=_params(("parallel",)),
        name="rwkv_post",
    )(y, bn, v, g, gn_g, gn_b, bd)


def _unpack_state(s):
    n, hp = s.shape[0], s.shape[1]
    return s.reshape(n, hp, HEAD, 2, HEAD).transpose(0, 1, 3, 2, 4).reshape(n, 2 * hp, HEAD, HEAD)


def _layer(x_prompt, x_sample, wkv0, conv0, shift0,
           ln_mix_pre, ln_mix_post, ln_ffn_pre, ln_ffn_post, w_in, b_gate,
           conv_w, conv_b, conv_ln_g, conv_ln_b, w_conv_out, shift_mu,
           w0, w2, a0, a2, g2, k_k, k_a, r_k, gn_g, gn_b, w_rwkv_out, w_o,
           w_ffn_gate, w_ffn_up, w_ffn_down):
    bp, tp, d = x_prompt.shape
    bs, ts, _ = x_sample.shape
    c = conv_w.shape[1]
    kw = conv_w.shape[0]
    dr = w2.shape[0]
    sw = shift_mu.shape[0]
    mp, ms = bp * tp, bs * ts
    assert tp % CHUNK == 0 and tp >= kw - 1, "prompt sequences are processed in 64-token chunks"
    tm = _pick(math.gcd(mp, ms), (512, 256, 128, 64, 32, 16))
    row = lambda x: x.reshape(1, -1)

    w_gates = _cast_cols(w_in, 2 * c + sw, 2 * d)
    zeros = jnp.zeros((dr, c), F32)
    wda = jnp.concatenate([jnp.concatenate([w2, zeros], axis=1),
                           jnp.concatenate([zeros, a2], axis=1)], axis=0).astype(BF16)
    idx = jnp.arange(LANES) // HEAD
    bd = (idx[:, None] == idx[None, :]).astype(BF16)

    x_p = x_prompt.reshape(mp, d)
    x_s = x_sample.reshape(ms, d)
    h = _rms_cast(x_p, x_s, row(ln_mix_pre), min(tm, 256))

    conv_args = (conv_w, row(conv_b), row(conv_ln_g), row(conv_ln_b))
    u = _mm_pair(_mm_glu_kernel, h, w_in, 0, w_in, c, c, F32, "mm_glu")
    u_s = u[mp:].reshape(bs, ts, c)
    zc_p = _conv_seq(u, bp, tp, *conv_args)
    ext_s = jnp.concatenate([conv0, u_s], axis=1)
    zc_s = _conv_step(ext_s.transpose(1, 0, 2), *conv_args).transpose(1, 0, 2).reshape(ms, c)
    conv_p = jnp.stack([u[(q + 1) * tp - (kw - 1):(q + 1) * tp] for q in range(bp)])
    conv_s = ext_s[:, ts:]

    pr = _mm(h, w_in, sw, 2 * c, name="mm_shift")
    pr_s = pr[mp:].reshape(bs, ts, sw)
    shifted_s = jnp.concatenate([shift0[:, None], pr_s[:, :-1]], axis=1).reshape(ms, sw)
    g2_bf = g2.astype(BF16)
    consts = (row(shift_mu), row(w0), row(a0), wda, g2_bf, row(k_k), row(k_a), row(r_k), bd)
    yr_p, st_p = _rwkv_chunk(pr, bp, tp, jnp.zeros((bp, c // LANES, HEAD, LANES), F32), row(shift_mu),
                             row(w0), row(a0), wda, g2_bf, row(k_k), row(k_a), row(r_k),
                             row(gn_g), row(gn_b), bd, c, dr)
    if bs % LANES == 0:
        pr_tm = pr_s.transpose(1, 0, 2)
        shifted_tm = jnp.concatenate([shift0[None], pr_tm[:-1]], axis=0)
        vec_s = _rwkv_prep(pr_tm.reshape(ms, sw), shifted_tm.reshape(ms, sw), consts, c, dr)
        y_s, st_t = _wkv_lanes(*vec_s[:6], wkv0.transpose(1, 2, 3, 0), ts)
        yr_tm = _rwkv_post(y_s, vec_s[7], vec_s[3], vec_s[6], row(gn_g), row(gn_b), bd)
        yr_s = yr_tm.reshape(ts, bs, c).transpose(1, 0, 2).reshape(ms, c)
        st_s = st_t.transpose(3, 0, 1, 2)
    else:
        vec_s = _rwkv_prep(pr_s.reshape(ms, sw), shifted_s, consts, c, dr)
        y_s, st_s = _wkv_step(*vec_s[:6], wkv0, bd, ts)
        yr_s = _rwkv_post(y_s, vec_s[7], vec_s[3], vec_s[6], row(gn_g), row(gn_b), bd)
    shift_p = jnp.concatenate([pr[(q + 1) * tp - 1:(q + 1) * tp] for q in range(bp)], axis=0)
    shift_s = pr_s[:, -1]

    mixed = _merge(h, zc_p, zc_s, yr_p, yr_s, w_gates, w_conv_out.astype(BF16),
                   w_rwkv_out.astype(BF16), b_gate[0:1], b_gate[1:2], tm)
    o = _mm(mixed, w_o, name="mm_o")
    x1, h2 = _rms_res(x_p, x_s, o, row(ln_mix_post), row(ln_ffn_pre), min(tm, 256))

    act = _mm_pair(_mm_swiglu_kernel, h2, w_ffn_gate, 0, w_ffn_up, 0, w_ffn_gate.shape[1], BF16, "mm_swiglu")
    f = _mm(act, w_ffn_down.astype(BF16), name="mm_down")
    tl = min(tm, 256)
    y_prompt = _rms_res_last(x1, f, row(ln_ffn_post), 0, mp, tl)
    y_sample = _rms_res_last(x1, f, row(ln_ffn_post), mp, ms, tl)

    return (y_prompt.reshape(bp, tp, d), y_sample.reshape(bs, ts, d),
            _unpack_state(st_p), conv_p, shift_p,
            st_s, conv_s, shift_s)


def kernel(x_prompt, x_sample, state_wkv, state_conv, state_shift, ln_mix_pre, ln_mix_post, ln_ffn_pre,
           ln_ffn_post, w_in, b_gate, conv_w, conv_b, conv_ln_g, conv_ln_b, w_conv_out, shift_mu, w0, w2,
           a0, a2, g2, k_k, k_a, r_k, gn_g, gn_b, w_rwkv_out, w_o, w_ffn_gate, w_ffn_up, w_ffn_down):
    depth = w_in.shape[0]
    assert depth == 1, "one decoder layer per step"
    weights = (ln_mix_pre, ln_mix_post, ln_ffn_pre, ln_ffn_post, w_in, b_gate,
               conv_w, conv_b, conv_ln_g, conv_ln_b, w_conv_out, shift_mu,
               w0, w2, a0, a2, g2, k_k, k_a, r_k, gn_g, gn_b, w_rwkv_out, w_o,
               w_ffn_gate, w_ffn_up, w_ffn_down)
    lw = tuple(wt[0] for wt in weights)
    yp, ys, wkv_p, conv_p, shift_p, wkv_s, conv_s, shift_s = _layer(
        x_prompt, x_sample, state_wkv[0], state_conv[0], state_shift[0], *lw)
    return (yp, ys, wkv_p[None], conv_p[None], shift_p[None],
            wkv_s[None], conv_s[None], shift_s[None])
```

```python
import functools
import math

import jax
import jax.numpy as jnp
from jax import lax
from jax.experimental import pallas as pl
from jax.experimental.pallas import tpu as pltpu

F32 = jnp.float32
BF16 = jnp.bfloat16

RMS_EPS = 1e-6
LN_EPS = 1e-5
GN_EPS = 64e-5
HEAD = 64
LANES = 128
SUBLANES = 8
CONV_HALO = 32
CHUNK = 64
SPLIT_FROM_LEVEL = 3
VMEM_LIMIT = 56 * 1024 * 1024


def _pick(n, cands):
    for c in cands:
        if n % c == 0:
            return c
    return n


def _params(sem):
    return pltpu.CompilerParams(dimension_semantics=sem, vmem_limit_bytes=VMEM_LIMIT)


def _dot(a, b):
    return jnp.dot(a, b, preferred_element_type=F32)


def _split(x):
    hi = x.astype(BF16)
    return hi, (x - hi.astype(F32)).astype(BF16)


def _mm_split_lhs(a, b):
    hi, lo = _split(a)
    bb = b.astype(BF16)
    return _dot(hi, bb) + _dot(lo, bb)


def _mm_split_both(a, b):
    ah, al = _split(a)
    bh, bl = _split(b)
    return _dot(ah, bh) + _dot(al, bh) + _dot(ah, bl)


def _head_sum(x, bd):
    cols = [_dot(x[:, c:c + LANES].astype(BF16), bd) for c in range(0, x.shape[1], LANES)]
    return jnp.concatenate(cols, axis=1)


def _two_group_specs(tm, d, nblk_p):
    return (pl.BlockSpec((tm, d), lambda i, *_: (jnp.minimum(i, nblk_p - 1), 0)),
            pl.BlockSpec((tm, d), lambda i, *_: (jnp.maximum(i - nblk_p, 0), 0)))


def _two_group_rows(p_ref, s_ref, nblk_p):
    return jnp.where(pl.program_id(0) < nblk_p, p_ref[...], s_ref[...])


def _rms_cast_kernel(xp_ref, xs_ref, g_ref, o_ref, *, nblk_p):
    x = _two_group_rows(xp_ref, xs_ref, nblk_p)
    ms = jnp.mean(x * x, axis=-1, keepdims=True)
    o_ref[...] = (x * lax.rsqrt(ms + RMS_EPS) * g_ref[...]).astype(o_ref.dtype)


def _rms_cast(x_p, x_s, g, tm):
    (mp, d), ms = x_p.shape, x_s.shape[0]
    xp_spec, xs_spec = _two_group_specs(tm, d, mp // tm)
    return pl.pallas_call(
        functools.partial(_rms_cast_kernel, nblk_p=mp // tm),
        grid=((mp + ms) // tm,),
        in_specs=[xp_spec, xs_spec, pl.BlockSpec((1, d), lambda i: (0, 0))],
        out_specs=pl.BlockSpec((tm, d), lambda i: (i, 0)),
        out_shape=jax.ShapeDtypeStruct((mp + ms, d), BF16),
        compiler_params=_params(("parallel",)),
        name="rms_cast",
    )(x_p, x_s, g)


def _rms_res_kernel(xp_ref, xs_ref, o_ref, g_ref, g2_ref, x1_ref, h_ref, *, nblk_p):
    o = o_ref[...].astype(F32)
    ms = jnp.mean(o * o, axis=-1, keepdims=True)
    x1 = _two_group_rows(xp_ref, xs_ref, nblk_p) + o * lax.rsqrt(ms + RMS_EPS) * g_ref[...]
    x1_ref[...] = x1
    ms1 = jnp.mean(x1 * x1, axis=-1, keepdims=True)
    h_ref[...] = (x1 * lax.rsqrt(ms1 + RMS_EPS) * g2_ref[...]).astype(h_ref.dtype)


def _rms_res(x_p, x_s, o, g, g2, tm):
    (mp, d), m = x_p.shape, o.shape[0]
    xp_spec, xs_spec = _two_group_specs(tm, d, mp // tm)
    row = pl.BlockSpec((tm, d), lambda i: (i, 0))
    vec = pl.BlockSpec((1, d), lambda i: (0, 0))
    return pl.pallas_call(
        functools.partial(_rms_res_kernel, nblk_p=mp // tm),
        grid=(m // tm,),
        in_specs=[xp_spec, xs_spec, row, vec, vec],
        out_specs=[row, row],
        out_shape=[jax.ShapeDtypeStruct((m, d), F32), jax.ShapeDtypeStruct((m, d), BF16)],
        compiler_params=_params(("parallel",)),
        name="rms_res",
    )(x_p, x_s, o, g, g2)


def _rms_res_last_kernel(x_ref, o_ref, g_ref, x1_ref):
    o = o_ref[...].astype(F32)
    ms = jnp.mean(o * o, axis=-1, keepdims=True)
    x1_ref[...] = x_ref[...] + o * lax.rsqrt(ms + RMS_EPS) * g_ref[...]


def _rms_res_last(x, o, g, row0, rows, tm):
    d = x.shape[1]
    blk0 = row0 // tm
    src = pl.BlockSpec((tm, d), lambda i: (blk0 + i, 0))
    return pl.pallas_call(
        _rms_res_last_kernel,
        grid=(rows // tm,),
        in_specs=[src, src, pl.BlockSpec((1, d), lambda i: (0, 0))],
        out_specs=pl.BlockSpec((tm, d), lambda i: (i, 0)),
        out_shape=jax.ShapeDtypeStruct((rows, d), F32),
        compiler_params=_params(("parallel",)),
        name="rms_res_last",
    )(x, o, g)


MM_VMEM_BUDGET = 50 * 1024 * 1024


def _mm_tiles(m, k, n, col0s, w_bytes, n_w, out_bytes):
    span = math.gcd(n, *col0s)
    tns = [t for t in (512, 256, 128) if span % t == 0] or [span]
    for tm in (1088, 1024, 544, 512, 256, 128, 64, 32, 16, 8):
        for tn in tns:
            need = (2 * tm * k * 2
                    + n_w * k * tn * (2 * w_bytes + (2 if w_bytes > 2 else 0))
                    + tm * tn * (2 * out_bytes + 4 * n_w))
            if m % tm == 0 and need <= MM_VMEM_BUDGET:
                return tm, tn
    return m, tns[-1]


def _mm_kernel(x_ref, w_ref, o_ref):
    o_ref[...] = _dot(x_ref[...], w_ref[...].astype(BF16)).astype(o_ref.dtype)


def _mm(x, w, n=None, col0=0, out_dtype=F32, name="mm"):
    m, k = x.shape
    n = w.shape[1] if n is None else n
    tm, tn = _mm_tiles(m, k, n, (col0,), w.dtype.itemsize, 1, jnp.dtype(out_dtype).itemsize)
    j0 = col0 // tn
    return pl.pallas_call(
        _mm_kernel,
        grid=(m // tm, n // tn),
        in_specs=[pl.BlockSpec((tm, k), lambda i, j: (i, 0)),
                  pl.BlockSpec((k, tn), lambda i, j: (0, j0 + j))],
        out_specs=pl.BlockSpec((tm, tn), lambda i, j: (i, j)),
        out_shape=jax.ShapeDtypeStruct((m, n), out_dtype),
        compiler_params=_params(("parallel", "arbitrary")),
        name=name,
    )(x, w)


def _mm_glu_kernel(x_ref, wa_ref, wb_ref, o_ref):
    x = x_ref[...]
    o_ref[...] = _dot(x, wa_ref[...].astype(BF16)) * jax.nn.sigmoid(_dot(x, wb_ref[...].astype(BF16)))


def _mm_swiglu_kernel(x_ref, wg_ref, wu_ref, o_ref):
    x = x_ref[...]
    o_ref[...] = (jax.nn.silu(_dot(x, wg_ref[...].astype(BF16)))
                  * _dot(x, wu_ref[...].astype(BF16))).astype(o_ref.dtype)


def _mm_pair(kern, x, wa, col_a, wb, col_b, n, out_dtype, name):
    m, k = x.shape
    tm, tn = _mm_tiles(m, k, n, (col_a, col_b), wa.dtype.itemsize, 2, jnp.dtype(out_dtype).itemsize)
    ja, jb = col_a // tn, col_b // tn
    return pl.pallas_call(
        kern,
        grid=(m // tm, n // tn),
        in_specs=[pl.BlockSpec((tm, k), lambda i, j: (i, 0)),
                  pl.BlockSpec((k, tn), lambda i, j: (0, ja + j)),
                  pl.BlockSpec((k, tn), lambda i, j: (0, jb + j))],
        out_specs=pl.BlockSpec((tm, tn), lambda i, j: (i, j)),
        out_shape=jax.ShapeDtypeStruct((m, n), out_dtype),
        compiler_params=_params(("parallel", "arbitrary")),
        name=name,
    )(x, wa, wb)


def _merge_kernel(h_ref, zcp_ref, zcs_ref, yrp_ref, yrs_ref, wg0_ref, wg1_ref, wc_ref, wr_ref,
                  b0_ref, b1_ref, o_ref, *, nblk_p):
    h = h_ref[...]
    g0 = jax.nn.sigmoid(_dot(h, wg0_ref[...]) + b0_ref[...])
    g1 = jax.nn.sigmoid(_dot(h, wg1_ref[...]) + b1_ref[...])
    oc = _dot(_two_group_rows(zcp_ref, zcs_ref, nblk_p), wc_ref[...])
    orr = _dot(_two_group_rows(yrp_ref, yrs_ref, nblk_p), wr_ref[...])
    o_ref[...] = (g0 * oc + g1 * orr).astype(o_ref.dtype)


def _cast_kernel(w_ref, o_ref):
    o_ref[...] = w_ref[...].astype(o_ref.dtype)


def _cast_cols(w, col0, n):
    k = w.shape[0]
    tn = _pick(math.gcd(n, col0), (256, 128))
    j0 = col0 // tn
    return pl.pallas_call(
        _cast_kernel,
        grid=(n // tn,),
        in_specs=[pl.BlockSpec((k, tn), lambda j: (0, j0 + j))],
        out_specs=pl.BlockSpec((k, tn), lambda j: (0, j)),
        out_shape=jax.ShapeDtypeStruct((k, n), BF16),
        compiler_params=_params(("parallel",)),
        name="cast_cols",
    )(w)


def _merge(h, zc_p, zc_s, yr_p, yr_s, wg, wc, wr, b0, b1, tm):
    m, d = h.shape
    mp, c = zc_p.shape
    tn = _pick(d, (256, 128))
    gp_spec, gs_spec = _two_group_specs(tm, c, mp // tm)
    return pl.pallas_call(
        functools.partial(_merge_kernel, nblk_p=mp // tm),
        grid=(m // tm, d // tn),
        in_specs=[pl.BlockSpec((tm, d), lambda i, j: (i, 0)),
                  gp_spec, gs_spec, gp_spec, gs_spec,
                  pl.BlockSpec((d, tn), lambda i, j: (0, j)),
                  pl.BlockSpec((d, tn), lambda i, j: (0, d // tn + j)),
                  pl.BlockSpec((c, tn), lambda i, j: (0, j)),
                  pl.BlockSpec((c, tn), lambda i, j: (0, j)),
                  pl.BlockSpec((1, tn), lambda i, j: (0, j)),
                  pl.BlockSpec((1, tn), lambda i, j: (0, j))],
        out_specs=pl.BlockSpec((tm, tn), lambda i, j: (i, j)),
        out_shape=jax.ShapeDtypeStruct((m, d), BF16),
        compiler_params=_params(("parallel", "arbitrary")),
        name="merge",
    )(h, zc_p, zc_s, yr_p, yr_s, wg, wg, wc, wr, b0, b1)


def _ln_silu(z, g, b):
    mu = jnp.mean(z, axis=-1, keepdims=True)
    zc = z - mu
    var = jnp.mean(zc * zc, axis=-1, keepdims=True)
    y = zc * lax.rsqrt(var + LN_EPS) * g + b
    return y * jax.nn.sigmoid(y)


def _conv_seq_kernel(um_ref, uh_ref, w_ref, cb_ref, g_ref, b_ref, o_ref, x_scr, xs_scr, z_scr,
                     *, tb, kw, lc):
    i = pl.program_id(1)
    c = um_ref.shape[1]
    halo = uh_ref[...]
    x_scr[0:CONV_HALO, :] = jnp.where(i == 0, jnp.zeros_like(halo), halo)
    x_scr[CONV_HALO:CONV_HALO + tb, :] = um_ref[...]
    off = CONV_HALO - (kw - 1)
    first = True
    for s in range(SUBLANES):
        taps = [j for j in range(kw) if (off + j) % SUBLANES == s]
        if not taps:
            continue
        span = max(off + j - s for j in taps) + tb
        xs_scr[0:span, :] = x_scr[s:s + span, :]
        for c0 in range(0, c, lc):
            cols = slice(c0, c0 + lc)
            acc = cb_ref[:, cols] if first else z_scr[:, cols]
            for j in taps:
                a0 = off + j - s
                acc = acc + xs_scr[a0:a0 + tb, cols] * w_ref[j:j + 1, cols]
            z_scr[:, cols] = acc
        first = False
    o_ref[...] = _ln_silu(z_scr[...], g_ref[...], b_ref[...]).astype(o_ref.dtype)


def _conv_seq(u, nseq, t, conv_w, conv_b, ln_g, ln_b):
    c = u.shape[1]
    kw = conv_w.shape[0]
    tb = _pick(t, (64, 32))
    lc = _pick(c, (256, 128))
    nb = t // tb
    hb = tb // CONV_HALO
    vec = pl.BlockSpec((1, c), lambda bi, i: (0, 0))
    return pl.pallas_call(
        functools.partial(_conv_seq_kernel, tb=tb, kw=kw, lc=lc),
        grid=(nseq, nb),
        in_specs=[pl.BlockSpec((tb, c), lambda bi, i: (bi * nb + i, 0)),
                  pl.BlockSpec((CONV_HALO, c), lambda bi, i: (jnp.maximum((bi * nb + i) * hb - 1, 0), 0)),
                  pl.BlockSpec((kw, c), lambda bi, i: (0, 0)),
                  vec, vec, vec],
        out_specs=pl.BlockSpec((tb, c), lambda bi, i: (bi * nb + i, 0)),
        out_shape=jax.ShapeDtypeStruct((nseq * t, c), BF16),
        scratch_shapes=[pltpu.VMEM((CONV_HALO + tb, c), F32), pltpu.VMEM((CONV_HALO + tb, c), F32),
                        pltpu.VMEM((tb, c), F32)],
        compiler_params=_params(("parallel", "arbitrary")),
        name="conv_seq",
    )(u, u, conv_w, conv_b, ln_g, ln_b)


def _conv_step_kernel(e_ref, w_ref, cb_ref, g_ref, b_ref, o_ref, *, kw, nt):
    for t in range(nt):
        acc = e_ref[t] * w_ref[0:1, :]
        for j in range(1, kw):
            acc = acc + e_ref[t + j] * w_ref[j:j + 1, :]
        o_ref[t] = _ln_silu(acc + cb_ref[...], g_ref[...], b_ref[...]).astype(o_ref.dtype)


def _conv_step(ext_tm, conv_w, conv_b, ln_g, ln_b):
    te, b, c = ext_tm.shape
    kw = conv_w.shape[0]
    nt = te - (kw - 1)
    sb = _pick(b, (8,))
    vec = pl.BlockSpec((1, c), lambda i: (0, 0))
    return pl.pallas_call(
        functools.partial(_conv_step_kernel, kw=kw, nt=nt),
        grid=(b // sb,),
        in_specs=[pl.BlockSpec((te, sb, c), lambda i: (0, i, 0)),
                  pl.BlockSpec((kw, c), lambda i: (0, 0)),
                  vec, vec, vec],
        out_specs=pl.BlockSpec((nt, sb, c), lambda i: (0, i, 0)),
        out_shape=jax.ShapeDtypeStruct((nt, b, c), BF16),
        compiler_params=_params(("parallel",)),
        name="conv_step",
    )(ext_tm, conv_w, conv_b, ln_g, ln_b)


def _rwkv_vector_math(m_r, m_k, m_v, m_low, w0, a0, wd_w, wd_a, g2, k_k, k_a, r_k, bd, dr):
    low = m_low[:, 0:2 * dr]
    lane = lax.broadcasted_iota(jnp.int32, low.shape, 1)
    low = jnp.where(lane < dr, jnp.tanh(low), low).astype(BF16)
    logw = -math.exp(-0.5) * jax.nn.sigmoid(w0 + _dot(low, wd_w))
    rate = jax.nn.sigmoid(a0 + _dot(low, wd_a))
    gate = _dot(jax.nn.sigmoid(m_low[:, 2 * dr:]).astype(BF16), g2)
    kk = m_k * k_k
    kk = kk / jnp.maximum(jnp.sqrt(_head_sum(kk * kk, bd)), 1e-12)
    k2 = m_k * (1.0 + (rate - 1.0) * k_a)
    bonus = _head_sum(m_r * k2 * r_k, bd)
    return m_r, logw, k2, m_v, -kk, kk * rate, gate, bonus


def _rwkv_prep_kernel(pr_ref, sh_ref, mu_ref, w0_ref, a0_ref, wda_ref, g2_ref, kk_ref, ka_ref, rk_ref,
                      bd_ref, *outs, c, dr):
    pr = pr_ref[...]
    m = pr + (sh_ref[...] - pr) * mu_ref[...]
    vals = _rwkv_vector_math(m[:, 0:c], m[:, c:2 * c], m[:, 2 * c:3 * c], m[:, 3 * c:],
                             w0_ref[...], a0_ref[...], wda_ref[:, 0:c], wda_ref[:, c:2 * c], g2_ref[...],
                             kk_ref[...], ka_ref[...], rk_ref[...], bd_ref[...], dr)
    for o_ref, val in zip(outs, vals):
        o_ref[...] = val


def _rwkv_prep(pr, shifted, consts, c, dr):
    rows, sw = pr.shape
    tm = _pick(rows, (128, 64, 32, 16, 8))
    row = pl.BlockSpec((tm, sw), lambda i: (i, 0))
    out = pl.BlockSpec((tm, c), lambda i: (i, 0))
    full = lambda x: pl.BlockSpec(x.shape, lambda i: (0,) * x.ndim)
    return pl.pallas_call(
        functools.partial(_rwkv_prep_kernel, c=c, dr=dr),
        grid=(rows // tm,),
        in_specs=[row, row] + [full(x) for x in consts],
        out_specs=[out] * 8,
        out_shape=[jax.ShapeDtypeStruct((rows, c), F32)] * 8,
        compiler_params=_params(("parallel",)),
        name="rwkv_prep",
    )(pr, shifted, *consts)


def _wkv_step_kernel(r_ref, w_ref, k_ref, v_ref, a_ref, b_ref, s0_ref, bd_ref, y_ref, st_ref,
                     *, nseq, t, g):
    bd = bd_ref[...]
    rows = lax.broadcasted_iota(jnp.int32, (HEAD, LANES), 0)
    lanes = lax.broadcasted_iota(jnp.int32, (HEAD, LANES), 1)
    diag = jnp.where(rows == lanes % HEAD, 1.0, 0.0).astype(F32)
    chains = [(q, p) for q in range(nseq) for p in range(g)]
    rr, kk, vv, aa, bb = (ref[...] for ref in (r_ref, k_ref, v_ref, a_ref, b_ref))
    ww = jnp.exp(w_ref[...])
    s = [jnp.concatenate([s0_ref[q, 2 * p], s0_ref[q, 2 * p + 1]], axis=1) for q, p in chains]
    yrows = {}
    for i in range(t):
        bc = lambda x, q, p: jnp.broadcast_to(
            x[q * t + i:q * t + i + 1, p * LANES:(p + 1) * LANES], (HEAD, LANES))
        lhs = jnp.concatenate(
            [jnp.concatenate([s[n] * bc(aa, q, p), bc(vv, q, p) * diag], axis=0)
             for n, (q, p) in enumerate(chains)], axis=0)
        res = _dot(lhs.astype(BF16), bd)
        s = [s[n] * bc(ww, q, p) + res[2 * n * HEAD:(2 * n + 1) * HEAD] * bc(bb, q, p)
             + res[(2 * n + 1) * HEAD:(2 * n + 2) * HEAD] * bc(kk, q, p)
             for n, (q, p) in enumerate(chains)]
        sr = jnp.concatenate([s[n] * bc(rr, q, p) for n, (q, p) in enumerate(chains)], axis=0)
        yc = _dot(sr.astype(BF16), bd)
        for n, (q, p) in enumerate(chains):
            yrows[(q, i, p)] = jnp.sum(yc[n * HEAD:(n + 1) * HEAD] * diag, axis=0, keepdims=True)
    y_ref[...] = jnp.concatenate(
        [jnp.concatenate([yrows[(q, i, p)] for p in range(g)], axis=1)
         for q in range(nseq) for i in range(t)], axis=0)
    for n, (q, p) in enumerate(chains):
        st_ref[q, 2 * p] = s[n][:, 0:HEAD]
        st_ref[q, 2 * p + 1] = s[n][:, HEAD:2 * HEAD]


def _wkv_step(r, lw, k, v, a, b, s0, bd, t):
    rows, c = r.shape
    npair = c // LANES
    assert SUBLANES % t == 0, "token-by-token path expects a few new tokens per sequence"
    nseq = SUBLANES // t
    g = _pick(npair, (8, 4, 2, 1))
    seq = pl.BlockSpec((nseq * t, g * LANES), lambda si, pi: (si, pi))
    st = pl.BlockSpec((nseq, 2 * g, HEAD, HEAD), lambda si, pi: (si, pi, 0, 0))
    return pl.pallas_call(
        functools.partial(_wkv_step_kernel, nseq=nseq, t=t, g=g),
        grid=(rows // (nseq * t), npair // g),
        in_specs=[seq] * 6 + [st, pl.BlockSpec((LANES, LANES), lambda si, pi: (0, 0))],
        out_specs=[seq, st],
        out_shape=[jax.ShapeDtypeStruct((rows, c), F32),
                   jax.ShapeDtypeStruct(s0.shape, F32)],
        compiler_params=_params(("parallel", "parallel")),
        name="wkv_step",
    )(r, lw, k, v, a, b, s0, bd)


def _wkv_lanes_kernel(r_ref, w_ref, k_ref, v_ref, a_ref, b_ref, s0_ref, y_ref, st_ref, xt_scr, yt_scr,
                      *, t, nb):
    names = (r_ref, w_ref, k_ref, v_ref, a_ref, b_ref)
    for i, ref in enumerate(names):
        for step in range(t):
            x = ref[step * nb:(step + 1) * nb, :].T
            xt_scr[i, step] = jnp.exp(x) if ref is w_ref else x
    ir, iw, ik, iv, ia, ib = range(6)
    sub = lax.broadcasted_iota(jnp.int32, (SUBLANES, nb), 0)
    for hh in range(2):
        ch = slice(hh * HEAD, (hh + 1) * HEAD)

        def group(vg, carry, hh=hh, ch=ch):
            v0 = pl.multiple_of(vg * SUBLANES, SUBLANES)
            vt = [xt_scr[iv, step, pl.ds(hh * HEAD + v0, SUBLANES), :] for step in range(t)]
            ytile = [jnp.zeros((SUBLANES, nb), F32) for _ in range(t)]
            for vi in range(SUBLANES):
                s = s0_ref[hh, v0 + vi]
                for step in range(t):
                    sa = jnp.sum(s * xt_scr[ia, step, ch, :], axis=0, keepdims=True)
                    s = (s * xt_scr[iw, step, ch, :] + sa * xt_scr[ib, step, ch, :]
                         + vt[step][vi:vi + 1, :] * xt_scr[ik, step, ch, :])
                    yrow = jnp.sum(s * xt_scr[ir, step, ch, :], axis=0, keepdims=True)
                    ytile[step] = jnp.where(sub == vi, yrow, ytile[step])
                st_ref[hh, v0 + vi] = s
            for step in range(t):
                yt_scr[step, pl.ds(hh * HEAD + v0, SUBLANES), :] = ytile[step]
            return carry

        lax.fori_loop(0, HEAD // SUBLANES, group, 0)
    for step in range(t):
        y_ref[step * nb:(step + 1) * nb, :] = yt_scr[step].T


def _wkv_lanes(r, lw, k, v, a, b, s0_t, t):
    rows, c = r.shape
    nb = rows // t
    seq = pl.BlockSpec((rows, LANES), lambda p: (0, p))
    st = pl.BlockSpec((2, HEAD, HEAD, nb), lambda p: (p, 0, 0, 0))
    return pl.pallas_call(
        functools.partial(_wkv_lanes_kernel, t=t, nb=nb),
        grid=(c // LANES,),
        in_specs=[seq] * 6 + [st],
        out_specs=[seq, st],
        out_shape=[jax.ShapeDtypeStruct((rows, c), F32), jax.ShapeDtypeStruct(s0_t.shape, F32)],
        scratch_shapes=[pltpu.VMEM((6, t, LANES, nb), F32), pltpu.VMEM((t, LANES, nb), F32)],
        compiler_params=_params(("parallel",)),
        name="wkv_lanes",
    )(r, lw, k, v, a, b, s0_t)


def _chunk_masks():
    n = 2 * CHUNK
    ri = jnp.arange(n)[:, None]
    ci = jnp.arange(n)[None, :]
    levels = [(ri // 2) == (ci // 2)]
    bsz = 2
    while bsz < CHUNK:
        levels.append(((ri // (2 * bsz)) == (ci // (2 * bsz))) & ((ri // bsz) != (ci // bsz)))
        bsz *= 2
    levels.append(ri == ci)
    lv = jnp.stack(levels).astype(F32)
    r4 = jnp.arange(2 * n)[:, None]
    c4 = jnp.arange(2 * n)[None, :]
    same_head = ((r4 // CHUNK) % 2) == ((c4 // CHUNK) % 2)
    t, s = r4 % CHUNK, c4 % CHUNK
    m1 = jnp.where(r4 < n, same_head & (s < t), same_head & (s <= t)).astype(F32)
    tri = (jnp.arange(CHUNK)[:, None] >= jnp.arange(CHUNK)[None, :]).astype(BF16)
    return lv, m1, tri


def _post_math(y, bonus, v, gate, gn_g, gn_b, bd):
    ym = _head_sum(y, bd) * (1.0 / HEAD)
    yc = y - ym
    yv = _head_sum(yc * yc, bd) * (1.0 / HEAD)
    yn = yc * lax.rsqrt(yv + GN_EPS) * gn_g + gn_b
    return (yn + bonus * v) * gate


def _rwkv_chunk_kernel(pr_r, pr_k, pr_v, pr_l, hr_ref, hk_ref, hv_ref, hl_ref, mu_r, mu_k, mu_v, mu_l,
                       w0_ref, a0_ref, wdw_ref, wda_ref, g2_ref, kk_ref, ka_ref, rk_ref, gg_ref, gb_ref,
                       s0_ref, lv_ref, m1_ref, tri_ref, bd_ref, y_ref, st_ref, *, g, dr):
    ci = pl.program_id(2)
    n = 2 * CHUNK
    lane = lax.broadcasted_iota(jnp.int32, (1, LANES), 1)
    m0 = jnp.where(lane < HEAD, 1.0, 0.0).astype(F32)
    m1 = 1.0 - m0
    expand = lambda x: jnp.concatenate([x * m0, x * m1], axis=0)
    fold = lambda x: x[0:CHUNK] + x[CHUNK:n]

    @pl.when(ci == 0)
    def _():
        st_ref[...] = s0_ref[...]

    def shifted_lerp(x_ref, h_ref, mu_ref):
        x = x_ref[...]
        prev = h_ref[SUBLANES - 1:SUBLANES, :]
        prev = jnp.where(ci == 0, jnp.zeros_like(prev), prev)
        row = lax.broadcasted_iota(jnp.int32, x.shape, 0)
        shifted = jnp.where(row == 0, jnp.broadcast_to(prev, x.shape), pltpu.roll(x, 1, 0))
        return x + (shifted - x) * mu_ref[...]

    bd = bd_ref[...]
    r_all, lw_all, k_all, v_all, a_all, b_all, gate, bonus = _rwkv_vector_math(
        shifted_lerp(pr_r, hr_ref, mu_r), shifted_lerp(pr_k, hk_ref, mu_k), shifted_lerp(pr_v, hv_ref, mu_v),
        shifted_lerp(pr_l, hl_ref, mu_l), w0_ref[...], a0_ref[...], wdw_ref[...], wda_ref[...], g2_ref[...],
        kk_ref[...], ka_ref[...], rk_ref[...], bd, dr)

    pairs = range(g)
    tri = tri_ref[...]
    mask1 = m1_ref[...]
    nlev = lv_ref.shape[0]
    lv = [lv_ref[i] for i in range(nlev)]
    sls = [slice(p * LANES, (p + 1) * LANES) for p in pairs]
    r, lw, k, v, a, b = ([x[:, sl] for sl in sls] for x in (r_all, lw_all, k_all, v_all, a_all, b_all))
    s_old = [expand(st_ref[0, p]) for p in pairs]
    lw_split = [_split(x) for x in lw]
    cum = [_dot(tri, hi) + _dot(tri, lo) for hi, lo in lw_split]
    cum_l = [x[CHUNK - 1:CHUNK] for x in cum]
    e_neg = [jnp.exp(-x) for x in cum]
    e_hat = [jnp.exp(cl - x) for cl, x in zip(cum_l, cum)]
    at_e = [expand(a[p] * jnp.exp(cum[p] - lw[p])) for p in pairs]
    rt_e = [expand(r[p] * jnp.exp(cum[p])) for p in pairs]
    v_e = [expand(x) for x in v]
    bt = [(b[p] * e_neg[p]).astype(BF16) for p in pairs]
    kt = [(k[p] * e_neg[p]).astype(BF16) for p in pairs]
    nt = (((1,), (1,)), ((), ()))
    out1 = [lax.dot_general(jnp.concatenate([at_e[p], rt_e[p]], axis=0).astype(BF16),
                            jnp.concatenate([bt[p], bt[p], kt[p], kt[p]], axis=0), nt,
                            preferred_element_type=F32) for p in pairs]
    out1 = [jnp.where(mask1 != 0.0, x, 0.0) for x in out1]
    a_ab = [x[0:n, 0:n] for x in out1]
    m_r = [x[n:2 * n, :].astype(BF16) for x in out1]
    akv = [_dot(out1[p][0:n, n:2 * n].astype(BF16), v_e[p].astype(BF16)) for p in pairs]
    tm = [lv[nlev - 1] + x * lv[0] for x in a_ab]
    for lev in range(1, nlev - 1):
        mm = _mm_split_lhs if lev > SPLIT_FROM_LEVEL else (lambda x, y: _dot(x.astype(BF16), y.astype(BF16)))
        step = [mm(tm[p], a_ab[p] * lv[lev]) for p in pairs]
        tm = [tm[p] + mm(step[p], tm[p]) for p in pairs]
    wu = [_mm_split_both(tm[p], jnp.concatenate([at_e[p], akv[p]], axis=1)) for p in pairs]
    rhs4 = [jnp.concatenate([wu[p], jnp.concatenate([jnp.zeros_like(v_e[p]), v_e[p]], axis=1)],
                            axis=0).astype(BF16) for p in pairs]
    o4 = [_dot(m_r[p], rhs4[p]) for p in pairs]
    q = [fold(rt_e[p] + o4[p][:, 0:LANES]).astype(BF16) for p in pairs]
    y1 = [fold(o4[p][:, LANES:2 * LANES]) for p in pairs]
    rhs5 = [jnp.concatenate([expand(b[p] * e_hat[p]), expand(k[p] * e_hat[p])], axis=0).astype(BF16)
            for p in pairs]
    o5 = [lax.dot_general(rhs4[p], rhs5[p], (((0,), (0,)), ((), ())), preferred_element_type=F32)
          for p in pairs]
    s_bf = [x.astype(BF16) for x in s_old]
    y = [lax.dot_general(q[p], s_bf[p], nt, preferred_element_type=F32) + y1[p] for p in pairs]
    s_new = [s_old[p] * jnp.exp(cum_l[p]) + _dot(s_bf[p], o5[p][0:n].astype(BF16)) + o5[p][n:2 * n]
             for p in pairs]
    y_all = jnp.concatenate(y, axis=1)
    y_ref[...] = _post_math(y_all, bonus, v_all, gate, gg_ref[...], gb_ref[...], bd).astype(y_ref.dtype)
    for p in pairs:
        st_ref[0, p] = fold(s_new[p])


def _rwkv_chunk(pr, nseq, t, s0, mu, w0, a0, wda, g2, k_k, k_a, r_k, gn_g, gn_b, bd, c, dr):
    sw = pr.shape[1]
    npair = c // LANES
    nc = t // CHUNK
    g = _pick(npair, (8, 4, 2, 1))
    gw = g * LANES
    lw_ = sw - 3 * c
    assert (3 * c) % lw_ == 0 and c % gw == 0
    hb = CHUNK // SUBLANES
    lv, m1, tri = _chunk_masks()
    row_blk = lambda si, ci: si * nc + ci
    halo_blk = lambda si, ci: jnp.maximum((si * nc + ci) * hb - 1, 0)
    wide = lambda part: pl.BlockSpec((CHUNK, gw), lambda si, pi, ci: (row_blk(si, ci), part * (c // gw) + pi))
    wide_h = lambda part: pl.BlockSpec((SUBLANES, gw),
                                       lambda si, pi, ci: (halo_blk(si, ci), part * (c // gw) + pi))
    low = pl.BlockSpec((CHUNK, lw_), lambda si, pi, ci: (row_blk(si, ci), 3 * c // lw_))
    low_h = pl.BlockSpec((SUBLANES, lw_), lambda si, pi, ci: (halo_blk(si, ci), 3 * c // lw_))
    vec = lambda part: pl.BlockSpec((1, gw), lambda si, pi, ci: (0, part * (c // gw) + pi))
    vec_low = pl.BlockSpec((1, lw_), lambda si, pi, ci: (0, 3 * c // lw_))
    cols = lambda rows, part: pl.BlockSpec((rows, gw), lambda si, pi, ci: (0, part * (c // gw) + pi))
    st = pl.BlockSpec((1, g, HEAD, LANES), lambda si, pi, ci: (si, pi, 0, 0))
    const = lambda x: pl.BlockSpec(x.shape, lambda si, pi, ci: (0,) * x.ndim)
    return pl.pallas_call(
        functools.partial(_rwkv_chunk_kernel, g=g, dr=dr),
        grid=(nseq, npair // g, nc),
        in_specs=[wide(0), wide(1), wide(2), low, wide_h(0), wide_h(1), wide_h(2), low_h,
                  vec(0), vec(1), vec(2), vec_low,
                  vec(0), vec(0), cols(2 * dr, 0), cols(2 * dr, 1), cols(g2.shape[0], 0),
                  vec(0), vec(0), vec(0), vec(0), vec(0),
                  st, const(lv), const(m1), const(tri), const(bd)],
        out_specs=[pl.BlockSpec((CHUNK, gw), lambda si, pi, ci: (row_blk(si, ci), pi)), st],
        out_shape=[jax.ShapeDtypeStruct((nseq * t, c), BF16),
                   jax.ShapeDtypeStruct((nseq, npair, HEAD, LANES), F32)],
        compiler_params=_params(("parallel", "parallel", "arbitrary")),
        name="rwkv_chunk",
    )(pr, pr, pr, pr, pr, pr, pr, pr, mu, mu, mu, mu, w0, a0, wda, wda, g2, k_k, k_a, r_k, gn_g, gn_b,
      s0, lv, m1, tri, bd)


def _rwkv_post_kernel(y_ref, bn_ref, v_ref, g_ref, gg_ref, gb_ref, bd_ref, o_ref):
    o_ref[...] = _post_math(y_ref[...], bn_ref[...], v_ref[...], g_ref[...], gg_ref[...], gb_ref[...],
                            bd_ref[...]).astype(o_ref.dtype)


def _rwkv_post(y, bn, v, g, gn_g, gn_b, bd):
    m, c = y.shape
    tm = _pick(m, (256, 128, 64, 32, 16, 8))
    row = pl.BlockSpec((tm, c), lambda i: (i, 0))
    vec = pl.BlockSpec((1, c), lambda i: (0, 0))
    return pl.pallas_call(
        _rwkv_post_kernel,
        grid=(m // tm,),
        in_specs=[row, row, row, row, vec, vec, pl.BlockSpec((LANES, LANES), lambda i: (0, 0))],
        out_specs=row,
        out_shape=jax.ShapeDtypeStruct((m, c), BF16),
        compiler_params=_params(("parallel",)),
        name="rwkv_post",
    )(y, bn, v, g, gn_g, gn_b, bd)


def _unpack_state(s):
    n, hp = s.shape[0], s.shape[1]
    return s.reshape(n, hp, HEAD, 2, HEAD).transpose(0, 1, 3, 2, 4).reshape(n, 2 * hp, HEAD, HEAD)


def _layer(x_prompt, x_sample, wkv0, conv0, shift0,
           ln_mix_pre, ln_mix_post, ln_ffn_pre, ln_ffn_post, w_in, b_gate,
           conv_w, conv_b, conv_ln_g, conv_ln_b, w_conv_out, shift_mu,
           w0, w2, a0, a2, g2, k_k, k_a, r_k, gn_g, gn_b, w_rwkv_out, w_o,
           w_ffn_gate, w_ffn_up, w_ffn_down):
    bp, tp, d = x_prompt.shape
    bs, ts, _ = x_sample.shape
    c = conv_w.shape[1]
    kw = conv_w.shape[0]
    dr = w2.shape[0]
    sw = shift_mu.shape[0]
    mp, ms = bp * tp, bs * ts
    assert tp % CHUNK == 0 and tp >= kw - 1, "prompt sequences are processed in 64-token chunks"
    tm = _pick(math.gcd(mp, ms), (512, 256, 128, 64, 32, 16))
    row = lambda x: x.reshape(1, -1)

    w_gates = _cast_cols(w_in, 2 * c + sw, 2 * d)
    zeros = jnp.zeros((dr, c), F32)
    wda = jnp.concatenate([jnp.concatenate([w2, zeros], axis=1),
                           jnp.concatenate([zeros, a2], axis=1)], axis=0).astype(BF16)
    idx = jnp.arange(LANES) // HEAD
    bd = (idx[:, None] == idx[None, :]).astype(BF16)

    x_p = x_prompt.reshape(mp, d)
    x_s = x_sample.reshape(ms, d)
    h = _rms_cast(x_p, x_s, row(ln_mix_pre), min(tm, 256))

    conv_args = (conv_w, row(conv_b), row(conv_ln_g), row(conv_ln_b))
    u = _mm_pair(_mm_glu_kernel, h, w_in, 0, w_in, c, c, F32, "mm_glu")
    u_s = u[mp:].reshape(bs, ts, c)
    zc_p = _conv_seq(u, bp, tp, *conv_args)
    ext_s = jnp.concatenate([conv0, u_s], axis=1)
    zc_s = _conv_step(ext_s.transpose(1, 0, 2), *conv_args).transpose(1, 0, 2).reshape(ms, c)
    conv_p = jnp.stack([u[(q + 1) * tp - (kw - 1):(q + 1) * tp] for q in range(bp)])
    conv_s = ext_s[:, ts:]

    pr = _mm(h, w_in, sw, 2 * c, name="mm_shift")
    pr_s = pr[mp:].reshape(bs, ts, sw)
    shifted_s = jnp.concatenate([shift0[:, None], pr_s[:, :-1]], axis=1).reshape(ms, sw)
    g2_bf = g2.astype(BF16)
    consts = (row(shift_mu), row(w0), row(a0), wda, g2_bf, row(k_k), row(k_a), row(r_k), bd)
    yr_p, st_p = _rwkv_chunk(pr, bp, tp, jnp.zeros((bp, c // LANES, HEAD, LANES), F32), row(shift_mu),
                             row(w0), row(a0), wda, g2_bf, row(k_k), row(k_a), row(r_k),
                             row(gn_g), row(gn_b), bd, c, dr)
    if bs % LANES == 0:
        pr_tm = pr_s.transpose(1, 0, 2)
        shifted_tm = jnp.concatenate([shift0[None], pr_tm[:-1]], axis=0)
        vec_s = _rwkv_prep(pr_tm.reshape(ms, sw), shifted_tm.reshape(ms, sw), consts, c, dr)
        y_s, st_t = _wkv_lanes(*vec_s[:6], wkv0.transpose(1, 2, 3, 0), ts)
        yr_tm = _rwkv_post(y_s, vec_s[7], vec_s[3], vec_s[6], row(gn_g), row(gn_b), bd)
        yr_s = yr_tm.reshape(ts, bs, c).transpose(1, 0, 2).reshape(ms, c)
        st_s = st_t.transpose(3, 0, 1, 2)
    else:
        vec_s = _rwkv_prep(pr_s.reshape(ms, sw), shifted_s, consts, c, dr)
        y_s, st_s = _wkv_step(*vec_s[:6], wkv0, bd, ts)
        yr_s = _rwkv_post(y_s, vec_s[7], vec_s[3], vec_s[6], row(gn_g), row(gn_b), bd)
    shift_p = jnp.concatenate([pr[(q + 1) * tp - 1:(q + 1) * tp] for q in range(bp)], axis=0)
    shift_s = pr_s[:, -1]

    mixed = _merge(h, zc_p, zc_s, yr_p, yr_s, w_gates, w_conv_out.astype(BF16),
                   w_rwkv_out.astype(BF16), b_gate[0:1], b_gate[1:2], tm)
    o = _mm(mixed, w_o, out_dtype=BF16, name="mm_o")
    x1, h2 = _rms_res(x_p, x_s, o, row(ln_mix_post), row(ln_ffn_pre), min(tm, 256))

    act = _mm_pair(_mm_swiglu_kernel, h2, w_ffn_gate, 0, w_ffn_up, 0, w_ffn_gate.shape[1], BF16, "mm_swiglu")
    f = _mm(act, w_ffn_down.astype(BF16), out_dtype=BF16, name="mm_down")
    tl = min(tm, 256)
    y_prompt = _rms_res_last(x1, f, row(ln_ffn_post), 0, mp, tl)
    y_sample = _rms_res_last(x1, f, row(ln_ffn_post), mp, ms, tl)

    return (y_prompt.reshape(bp, tp, d), y_sample.reshape(bs, ts, d),
            _unpack_state(st_p), conv_p, shift_p,
            st_s, conv_s, shift_s)


def kernel(x_prompt, x_sample, state_wkv, state_conv, state_shift, ln_mix_pre, ln_mix_post, ln_ffn_pre,
           ln_ffn_post, w_in, b_gate, conv_w, conv_b, conv_ln_g, conv_ln_b, w_conv_out, shift_mu, w0, w2,
           a0, a2, g2, k_k, k_a, r_k, gn_g, gn_b, w_rwkv_out, w_o, w_ffn_gate, w_ffn_up, w_ffn_down):
    depth = w_in.shape[0]
    assert depth == 1, "one decoder layer per step"
    weights = (ln_mix_pre, ln_mix_post, ln_ffn_pre, ln_ffn_post, w_in, b_gate,
               conv_w, conv_b, conv_ln_g, conv_ln_b, w_conv_out, shift_mu,
               w0, w2, a0, a2, g2, k_k, k_a, r_k, gn_g, gn_b, w_rwkv_out, w_o,
               w_ffn_gate, w_ffn_up, w_ffn_down)
    lw = tuple(wt[0] for wt in weights)
    yp, ys, wkv_p, conv_p, shift_p, wkv_s, conv_s, shift_s = _layer(
        x_prompt, x_sample, state_wkv[0], state_conv[0], state_shift[0], *lw)
    return (yp, ys, wkv_p[None], conv_p[None], shift_p[None],
            wkv_s[None], conv_s[None], shift_s[None])
```

```python
import functools
import math

import jax
import jax.numpy as jnp
from jax import lax
from jax.experimental import pallas as pl
from jax.experimental.pallas import tpu as pltpu

F32 = jnp.float32
BF16 = jnp.bfloat16

RMS_EPS = 1e-6
LN_EPS = 1e-5
GN_EPS = 64e-5
HEAD = 64
LANES = 128
SUBLANES = 8
CONV_HALO = 32
CHUNK = 64
SPLIT_FROM_LEVEL = 3
VMEM_LIMIT = 56 * 1024 * 1024


def _pick(n, cands):
    for c in cands:
        if n % c == 0:
            return c
    return n


def _params(sem):
    return pltpu.CompilerParams(dimension_semantics=sem, vmem_limit_bytes=VMEM_LIMIT)


def _dot(a, b):
    return jnp.dot(a, b, preferred_element_type=F32)


def _split(x):
    hi = x.astype(BF16)
    return hi, (x - hi.astype(F32)).astype(BF16)


def _mm_split_lhs(a, b):
    hi, lo = _split(a)
    bb = b.astype(BF16)
    return _dot(hi, bb) + _dot(lo, bb)


def _mm_split_both(a, b):
    ah, al = _split(a)
    bh, bl = _split(b)
    return _dot(ah, bh) + _dot(al, bh) + _dot(ah, bl)


def _head_sum(x, bd):
    cols = [_dot(x[:, c:c + LANES].astype(BF16), bd) for c in range(0, x.shape[1], LANES)]
    return jnp.concatenate(cols, axis=1)


def _two_group_specs(tm, d, nblk_p):
    return (pl.BlockSpec((tm, d), lambda i, *_: (jnp.minimum(i, nblk_p - 1), 0)),
            pl.BlockSpec((tm, d), lambda i, *_: (jnp.maximum(i - nblk_p, 0), 0)))


def _two_group_rows(p_ref, s_ref, nblk_p):
    return jnp.where(pl.program_id(0) < nblk_p, p_ref[...], s_ref[...])


def _rms_cast_kernel(xp_ref, xs_ref, g_ref, o_ref, *, nblk_p):
    x = _two_group_rows(xp_ref, xs_ref, nblk_p)
    ms = jnp.mean(x * x, axis=-1, keepdims=True)
    o_ref[...] = (x * lax.rsqrt(ms + RMS_EPS) * g_ref[...]).astype(o_ref.dtype)


def _rms_cast(x_p, x_s, g, tm):
    (mp, d), ms = x_p.shape, x_s.shape[0]
    xp_spec, xs_spec = _two_group_specs(tm, d, mp // tm)
    return pl.pallas_call(
        functools.partial(_rms_cast_kernel, nblk_p=mp // tm),
        grid=((mp + ms) // tm,),
        in_specs=[xp_spec, xs_spec, pl.BlockSpec((1, d), lambda i: (0, 0))],
        out_specs=pl.BlockSpec((tm, d), lambda i: (i, 0)),
        out_shape=jax.ShapeDtypeStruct((mp + ms, d), BF16),
        compiler_params=_params(("parallel",)),
        name="rms_cast",
    )(x_p, x_s, g)


def _rms_res_kernel(xp_ref, xs_ref, o_ref, g_ref, g2_ref, x1_ref, h_ref, *, nblk_p):
    o = o_ref[...].astype(F32)
    ms = jnp.mean(o * o, axis=-1, keepdims=True)
    x1 = _two_group_rows(xp_ref, xs_ref, nblk_p) + o * lax.rsqrt(ms + RMS_EPS) * g_ref[...]
    x1_ref[...] = x1
    ms1 = jnp.mean(x1 * x1, axis=-1, keepdims=True)
    h_ref[...] = (x1 * lax.rsqrt(ms1 + RMS_EPS) * g2_ref[...]).astype(h_ref.dtype)


def _rms_res(x_p, x_s, o, g, g2, tm):
    (mp, d), m = x_p.shape, o.shape[0]
    xp_spec, xs_spec = _two_group_specs(tm, d, mp // tm)
    row = pl.BlockSpec((tm, d), lambda i: (i, 0))
    vec = pl.BlockSpec((1, d), lambda i: (0, 0))
    return pl.pallas_call(
        functools.partial(_rms_res_kernel, nblk_p=mp // tm),
        grid=(m // tm,),
        in_specs=[xp_spec, xs_spec, row, vec, vec],
        out_specs=[row, row],
        out_shape=[jax.ShapeDtypeStruct((m, d), F32), jax.ShapeDtypeStruct((m, d), BF16)],
        compiler_params=_params(("parallel",)),
        name="rms_res",
    )(x_p, x_s, o, g, g2)


def _rms_res_last_kernel(x_ref, o_ref, g_ref, x1_ref):
    o = o_ref[...].astype(F32)
    ms = jnp.mean(o * o, axis=-1, keepdims=True)
    x1_ref[...] = x_ref[...] + o * lax.rsqrt(ms + RMS_EPS) * g_ref[...]


def _rms_res_last(x, o, g, row0, rows, tm):
    d = x.shape[1]
    blk0 = row0 // tm
    src = pl.BlockSpec((tm, d), lambda i: (blk0 + i, 0))
    return pl.pallas_call(
        _rms_res_last_kernel,
        grid=(rows // tm,),
        in_specs=[src, src, pl.BlockSpec((1, d), lambda i: (0, 0))],
        out_specs=pl.BlockSpec((tm, d), lambda i: (i, 0)),
        out_shape=jax.ShapeDtypeStruct((rows, d), F32),
        compiler_params=_params(("parallel",)),
        name="rms_res_last",
    )(x, o, g)


MM_VMEM_BUDGET = 50 * 1024 * 1024


def _mm_tiles(m, k, n, col0s, w_bytes, n_w, out_bytes):
    span = math.gcd(n, *col0s)
    tns = [t for t in (512, 256, 128) if span % t == 0] or [span]
    for tm in (1088, 1024, 544, 512, 256, 128, 64, 32, 16, 8):
        for tn in tns:
            need = (2 * tm * k * 2
                    + n_w * k * tn * (2 * w_bytes + (2 if w_bytes > 2 else 0))
                    + tm * tn * (2 * out_bytes + 4 * n_w))
            if m % tm == 0 and need <= MM_VMEM_BUDGET:
                return tm, tn
    return m, tns[-1]


def _mm_kernel(x_ref, w_ref, o_ref):
    o_ref[...] = _dot(x_ref[...], w_ref[...].astype(BF16)).astype(o_ref.dtype)


def _mm(x, w, n=None, col0=0, out_dtype=F32, name="mm"):
    m, k = x.shape
    n = w.shape[1] if n is None else n
    tm, tn = _mm_tiles(m, k, n, (col0,), w.dtype.itemsize, 1, jnp.dtype(out_dtype).itemsize)
    j0 = col0 // tn
    return pl.pallas_call(
        _mm_kernel,
        grid=(m // tm, n // tn),
        in_specs=[pl.BlockSpec((tm, k), lambda i, j: (i, 0)),
                  pl.BlockSpec((k, tn), lambda i, j: (0, j0 + j))],
        out_specs=pl.BlockSpec((tm, tn), lambda i, j: (i, j)),
        out_shape=jax.ShapeDtypeStruct((m, n), out_dtype),
        compiler_params=_params(("parallel", "arbitrary")),
        name=name,
    )(x, w)


def _mm_glu_kernel(x_ref, wa_ref, wb_ref, o_ref):
    x = x_ref[...]
    o_ref[...] = _dot(x, wa_ref[...].astype(BF16)) * jax.nn.sigmoid(_dot(x, wb_ref[...].astype(BF16)))


def _mm_swiglu_kernel(x_ref, wg_ref, wu_ref, o_ref):
    x = x_ref[...]
    o_ref[...] = (jax.nn.silu(_dot(x, wg_ref[...].astype(BF16)))
                  * _dot(x, wu_ref[...].astype(BF16))).astype(o_ref.dtype)


def _mm_pair(kern, x, wa, col_a, wb, col_b, n, out_dtype, name):
    m, k = x.shape
    tm, tn = _mm_tiles(m, k, n, (col_a, col_b), wa.dtype.itemsize, 2, jnp.dtype(out_dtype).itemsize)
    ja, jb = col_a // tn, col_b // tn
    return pl.pallas_call(
        kern,
        grid=(m // tm, n // tn),
        in_specs=[pl.BlockSpec((tm, k), lambda i, j: (i, 0)),
                  pl.BlockSpec((k, tn), lambda i, j: (0, ja + j)),
                  pl.BlockSpec((k, tn), lambda i, j: (0, jb + j))],
        out_specs=pl.BlockSpec((tm, tn), lambda i, j: (i, j)),
        out_shape=jax.ShapeDtypeStruct((m, n), out_dtype),
        compiler_params=_params(("parallel", "arbitrary")),
        name=name,
    )(x, wa, wb)


def _merge_kernel(h_ref, zcp_ref, zcs_ref, yrp_ref, yrs_ref, wg0_ref, wg1_ref, wc_ref, wr_ref,
                  b0_ref, b1_ref, o_ref, *, nblk_p):
    h = h_ref[...]
    g0 = jax.nn.sigmoid(_dot(h, wg0_ref[0]) + b0_ref[...])
    g1 = jax.nn.sigmoid(_dot(h, wg1_ref[0]) + b1_ref[...])
    oc = _dot(_two_group_rows(zcp_ref, zcs_ref, nblk_p), wc_ref[0])
    orr = _dot(_two_group_rows(yrp_ref, yrs_ref, nblk_p), wr_ref[0])
    o_ref[...] = (g0 * oc + g1 * orr).astype(o_ref.dtype)


def _cast_kernel(w_ref, o_ref):
    o_ref[0] = w_ref[...].astype(o_ref.dtype)


def _cast_tiles(w, col0, n, tn):
    k = w.shape[0]
    assert col0 % tn == 0 and n % tn == 0
    j0 = col0 // tn
    tk = _pick(k, (4096, 2752, 2048, 1024, 512, 256, 128, 64, 32, 16))
    return pl.pallas_call(
        _cast_kernel,
        grid=(n // tn, k // tk),
        in_specs=[pl.BlockSpec((tk, tn), lambda j, kb: (kb, j0 + j))],
        out_specs=pl.BlockSpec((1, tk, tn), lambda j, kb: (j, kb, 0)),
        out_shape=jax.ShapeDtypeStruct((n // tn, k, tn), BF16),
        compiler_params=_params(("parallel", "parallel")),
        name="cast_tiles",
    )(w)


def _mm_tiled_kernel(x_ref, w_ref, o_ref):
    o_ref[...] = _dot(x_ref[...], w_ref[0]).astype(o_ref.dtype)


def _mm_tiled(x, wt, out_dtype, name):
    m, k = x.shape
    nt, _, tn = wt.shape
    out_bytes = jnp.dtype(out_dtype).itemsize
    tm = next(t for t in (1088, 1024, 544, 512, 256, 128, 64, 32, 16, 8, m) if m % t == 0
              and 2 * t * k * 2 + 2 * k * tn * 2 + t * tn * (2 * out_bytes + 4) <= MM_VMEM_BUDGET)
    return pl.pallas_call(
        _mm_tiled_kernel,
        grid=(m // tm, nt),
        in_specs=[pl.BlockSpec((tm, k), lambda i, j: (i, 0)),
                  pl.BlockSpec((1, k, tn), lambda i, j: (j, 0, 0))],
        out_specs=pl.BlockSpec((tm, tn), lambda i, j: (i, j)),
        out_shape=jax.ShapeDtypeStruct((m, nt * tn), out_dtype),
        compiler_params=_params(("parallel", "arbitrary")),
        name=name,
    )(x, wt)


def _merge(h, zc_p, zc_s, yr_p, yr_s, wg, wc, wr, b0, b1, tm):
    m, d = h.shape
    mp, c = zc_p.shape
    tn = wg.shape[2]
    gp_spec, gs_spec = _two_group_specs(tm, c, mp // tm)
    return pl.pallas_call(
        functools.partial(_merge_kernel, nblk_p=mp // tm),
        grid=(m // tm, d // tn),
        in_specs=[pl.BlockSpec((tm, d), lambda i, j: (i, 0)),
                  gp_spec, gs_spec, gp_spec, gs_spec,
                  pl.BlockSpec((1, d, tn), lambda i, j: (j, 0, 0)),
                  pl.BlockSpec((1, d, tn), lambda i, j: (d // tn + j, 0, 0)),
                  pl.BlockSpec((1, c, tn), lambda i, j: (j, 0, 0)),
                  pl.BlockSpec((1, c, tn), lambda i, j: (j, 0, 0)),
                  pl.BlockSpec((1, tn), lambda i, j: (0, j)),
                  pl.BlockSpec((1, tn), lambda i, j: (0, j))],
        out_specs=pl.BlockSpec((tm, tn), lambda i, j: (i, j)),
        out_shape=jax.ShapeDtypeStruct((m, d), BF16),
        compiler_params=_params(("parallel", "arbitrary")),
        name="merge",
    )(h, zc_p, zc_s, yr_p, yr_s, wg, wg, wc, wr, b0, b1)


def _ln_silu(z, g, b):
    mu = jnp.mean(z, axis=-1, keepdims=True)
    zc = z - mu
    var = jnp.mean(zc * zc, axis=-1, keepdims=True)
    y = zc * lax.rsqrt(var + LN_EPS) * g + b
    return y * jax.nn.sigmoid(y)


def _conv_seq_kernel(um_ref, uh_ref, w_ref, cb_ref, g_ref, b_ref, o_ref, x_scr, xs_scr, z_scr,
                     *, tb, kw, lc):
    i = pl.program_id(1)
    c = um_ref.shape[1]
    halo = uh_ref[...]
    x_scr[0:CONV_HALO, :] = jnp.where(i == 0, jnp.zeros_like(halo), halo)
    x_scr[CONV_HALO:CONV_HALO + tb, :] = um_ref[...]
    off = CONV_HALO - (kw - 1)
    first = True
    for s in range(SUBLANES):
        taps = [j for j in range(kw) if (off + j) % SUBLANES == s]
        if not taps:
            continue
        span = max(off + j - s for j in taps) + tb
        xs_scr[0:span, :] = x_scr[s:s + span, :]
        for c0 in range(0, c, lc):
            cols = slice(c0, c0 + lc)
            acc = cb_ref[:, cols] if first else z_scr[:, cols]
            for j in taps:
                a0 = off + j - s
                acc = acc + xs_scr[a0:a0 + tb, cols] * w_ref[j:j + 1, cols]
            z_scr[:, cols] = acc
        first = False
    o_ref[...] = _ln_silu(z_scr[...], g_ref[...], b_ref[...]).astype(o_ref.dtype)


def _conv_seq(u, nseq, t, conv_w, conv_b, ln_g, ln_b):
    c = u.shape[1]
    kw = conv_w.shape[0]
    tb = _pick(t, (128, 64, 32))
    lc = _pick(c, (128,))
    nb = t // tb
    hb = tb // CONV_HALO
    vec = pl.BlockSpec((1, c), lambda bi, i: (0, 0))
    return pl.pallas_call(
        functools.partial(_conv_seq_kernel, tb=tb, kw=kw, lc=lc),
        grid=(nseq, nb),
        in_specs=[pl.BlockSpec((tb, c), lambda bi, i: (bi * nb + i, 0)),
                  pl.BlockSpec((CONV_HALO, c), lambda bi, i: (jnp.maximum((bi * nb + i) * hb - 1, 0), 0)),
                  pl.BlockSpec((kw, c), lambda bi, i: (0, 0)),
                  vec, vec, vec],
        out_specs=pl.BlockSpec((tb, c), lambda bi, i: (bi * nb + i, 0)),
        out_shape=jax.ShapeDtypeStruct((nseq * t, c), BF16),
        scratch_shapes=[pltpu.VMEM((CONV_HALO + tb, c), F32), pltpu.VMEM((CONV_HALO + tb, c), F32),
                        pltpu.VMEM((tb, c), F32)],
        compiler_params=_params(("parallel", "arbitrary")),
        name="conv_seq",
    )(u, u, conv_w, conv_b, ln_g, ln_b)


def _conv_step_kernel(e_ref, w_ref, cb_ref, g_ref, b_ref, o_ref, *, kw, nt):
    for t in range(nt):
        acc = e_ref[t] * w_ref[0:1, :]
        for j in range(1, kw):
            acc = acc + e_ref[t + j] * w_ref[j:j + 1, :]
        o_ref[t] = _ln_silu(acc + cb_ref[...], g_ref[...], b_ref[...]).astype(o_ref.dtype)


def _conv_step(ext_tm, conv_w, conv_b, ln_g, ln_b):
    te, b, c = ext_tm.shape
    kw = conv_w.shape[0]
    nt = te - (kw - 1)
    sb = _pick(b, (8,))
    vec = pl.BlockSpec((1, c), lambda i: (0, 0))
    return pl.pallas_call(
        functools.partial(_conv_step_kernel, kw=kw, nt=nt),
        grid=(b // sb,),
        in_specs=[pl.BlockSpec((te, sb, c), lambda i: (0, i, 0)),
                  pl.BlockSpec((kw, c), lambda i: (0, 0)),
                  vec, vec, vec],
        out_specs=pl.BlockSpec((nt, sb, c), lambda i: (0, i, 0)),
        out_shape=jax.ShapeDtypeStruct((nt, b, c), BF16),
        compiler_params=_params(("parallel",)),
        name="conv_step",
    )(ext_tm, conv_w, conv_b, ln_g, ln_b)


def _rwkv_vector_math(m_r, m_k, m_v, m_low, w0, a0, wd_w, wd_a, g2, k_k, k_a, r_k, bd, dr):
    low = m_low[:, 0:2 * dr]
    lane = lax.broadcasted_iota(jnp.int32, low.shape, 1)
    low = jnp.where(lane < dr, jnp.tanh(low), low).astype(BF16)
    logw = -math.exp(-0.5) * jax.nn.sigmoid(w0 + _dot(low, wd_w))
    rate = jax.nn.sigmoid(a0 + _dot(low, wd_a))
    gate = _dot(jax.nn.sigmoid(m_low[:, 2 * dr:]).astype(BF16), g2)
    kk = m_k * k_k
    kk = kk / jnp.maximum(jnp.sqrt(_head_sum(kk * kk, bd)), 1e-12)
    k2 = m_k * (1.0 + (rate - 1.0) * k_a)
    bonus = _head_sum(m_r * k2 * r_k, bd)
    return m_r, logw, k2, m_v, -kk, kk * rate, gate, bonus


def _rwkv_prep_kernel(pr_ref, sh_ref, mu_ref, w0_ref, a0_ref, wda_ref, g2_ref, kk_ref, ka_ref, rk_ref,
                      bd_ref, *outs, c, dr):
    pr = pr_ref[...]
    m = pr + (sh_ref[...] - pr) * mu_ref[...]
    vals = _rwkv_vector_math(m[:, 0:c], m[:, c:2 * c], m[:, 2 * c:3 * c], m[:, 3 * c:],
                             w0_ref[...], a0_ref[...], wda_ref[:, 0:c], wda_ref[:, c:2 * c], g2_ref[...],
                             kk_ref[...], ka_ref[...], rk_ref[...], bd_ref[...], dr)
    for o_ref, val in zip(outs, vals):
        o_ref[...] = val


def _rwkv_prep(pr, shifted, consts, c, dr):
    rows, sw = pr.shape
    tm = _pick(rows, (128, 64, 32, 16, 8))
    row = pl.BlockSpec((tm, sw), lambda i: (i, 0))
    out = pl.BlockSpec((tm, c), lambda i: (i, 0))
    full = lambda x: pl.BlockSpec(x.shape, lambda i: (0,) * x.ndim)
    return pl.pallas_call(
        functools.partial(_rwkv_prep_kernel, c=c, dr=dr),
        grid=(rows // tm,),
        in_specs=[row, row] + [full(x) for x in consts],
        out_specs=[out] * 8,
        out_shape=[jax.ShapeDtypeStruct((rows, c), F32)] * 8,
        compiler_params=_params(("parallel",)),
        name="rwkv_prep",
    )(pr, shifted, *consts)


def _wkv_step_kernel(r_ref, w_ref, k_ref, v_ref, a_ref, b_ref, s0_ref, bd_ref, y_ref, st_ref,
                     *, nseq, t, g):
    bd = bd_ref[...]
    rows = lax.broadcasted_iota(jnp.int32, (HEAD, LANES), 0)
    lanes = lax.broadcasted_iota(jnp.int32, (HEAD, LANES), 1)
    diag = jnp.where(rows == lanes % HEAD, 1.0, 0.0).astype(F32)
    chains = [(q, p) for q in range(nseq) for p in range(g)]
    rr, kk, vv, aa, bb = (ref[...] for ref in (r_ref, k_ref, v_ref, a_ref, b_ref))
    ww = jnp.exp(w_ref[...])
    s = [jnp.concatenate([s0_ref[q, 2 * p], s0_ref[q, 2 * p + 1]], axis=1) for q, p in chains]
    yrows = {}
    for i in range(t):
        bc = lambda x, q, p: jnp.broadcast_to(
            x[q * t + i:q * t + i + 1, p * LANES:(p + 1) * LANES], (HEAD, LANES))
        lhs = jnp.concatenate(
            [jnp.concatenate([s[n] * bc(aa, q, p), bc(vv, q, p) * diag], axis=0)
             for n, (q, p) in enumerate(chains)], axis=0)
        res = _dot(lhs.astype(BF16), bd)
        s = [s[n] * bc(ww, q, p) + res[2 * n * HEAD:(2 * n + 1) * HEAD] * bc(bb, q, p)
             + res[(2 * n + 1) * HEAD:(2 * n + 2) * HEAD] * bc(kk, q, p)
             for n, (q, p) in enumerate(chains)]
        sr = jnp.concatenate([s[n] * bc(rr, q, p) for n, (q, p) in enumerate(chains)], axis=0)
        yc = _dot(sr.astype(BF16), bd)
        for n, (q, p) in enumerate(chains):
            yrows[(q, i, p)] = jnp.sum(yc[n * HEAD:(n + 1) * HEAD] * diag, axis=0, keepdims=True)
    y_ref[...] = jnp.concatenate(
        [jnp.concatenate([yrows[(q, i, p)] for p in range(g)], axis=1)
         for q in range(nseq) for i in range(t)], axis=0)
    for n, (q, p) in enumerate(chains):
        st_ref[q, 2 * p] = s[n][:, 0:HEAD]
        st_ref[q, 2 * p + 1] = s[n][:, HEAD:2 * HEAD]


def _wkv_step(r, lw, k, v, a, b, s0, bd, t):
    rows, c = r.shape
    npair = c // LANES
    assert SUBLANES % t == 0, "token-by-token path expects a few new tokens per sequence"
    nseq = SUBLANES // t
    g = _pick(npair, (8, 4, 2, 1))
    seq = pl.BlockSpec((nseq * t, g * LANES), lambda si, pi: (si, pi))
    st = pl.BlockSpec((nseq, 2 * g, HEAD, HEAD), lambda si, pi: (si, pi, 0, 0))
    return pl.pallas_call(
        functools.partial(_wkv_step_kernel, nseq=nseq, t=t, g=g),
        grid=(rows // (nseq * t), npair // g),
        in_specs=[seq] * 6 + [st, pl.BlockSpec((LANES, LANES), lambda si, pi: (0, 0))],
        out_specs=[seq, st],
        out_shape=[jax.ShapeDtypeStruct((rows, c), F32),
                   jax.ShapeDtypeStruct(s0.shape, F32)],
        compiler_params=_params(("parallel", "parallel")),
        name="wkv_step",
    )(r, lw, k, v, a, b, s0, bd)


def _wkv_lanes_kernel(r_ref, w_ref, k_ref, v_ref, a_ref, b_ref, s0_ref, y_ref, st_ref, xt_scr, yt_scr,
                      *, t, nb):
    names = (r_ref, w_ref, k_ref, v_ref, a_ref, b_ref)
    for i, ref in enumerate(names):
        for step in range(t):
            x = ref[step * nb:(step + 1) * nb, :].T
            xt_scr[i, step] = jnp.exp(x) if ref is w_ref else x
    ir, iw, ik, iv, ia, ib = range(6)
    sub = lax.broadcasted_iota(jnp.int32, (SUBLANES, nb), 0)
    for hh in range(2):
        ch = slice(hh * HEAD, (hh + 1) * HEAD)

        def group(vg, carry, hh=hh, ch=ch):
            v0 = pl.multiple_of(vg * SUBLANES, SUBLANES)
            vt = [xt_scr[iv, step, pl.ds(hh * HEAD + v0, SUBLANES), :] for step in range(t)]
            ytile = [jnp.zeros((SUBLANES, nb), F32) for _ in range(t)]
            for vi in range(SUBLANES):
                s = s0_ref[hh, v0 + vi]
                for step in range(t):
                    sa = jnp.sum(s * xt_scr[ia, step, ch, :], axis=0, keepdims=True)
                    s = (s * xt_scr[iw, step, ch, :] + sa * xt_scr[ib, step, ch, :]
                         + vt[step][vi:vi + 1, :] * xt_scr[ik, step, ch, :])
                    yrow = jnp.sum(s * xt_scr[ir, step, ch, :], axis=0, keepdims=True)
                    ytile[step] = jnp.where(sub == vi, yrow, ytile[step])
                st_ref[hh, v0 + vi] = s
            for step in range(t):
                yt_scr[step, pl.ds(hh * HEAD + v0, SUBLANES), :] = ytile[step]
            return carry

        lax.fori_loop(0, HEAD // SUBLANES, group, 0)
    for step in range(t):
        y_ref[step * nb:(step + 1) * nb, :] = yt_scr[step].T


def _wkv_lanes(r, lw, k, v, a, b, s0_t, t):
    rows, c = r.shape
    nb = rows // t
    seq = pl.BlockSpec((rows, LANES), lambda p: (0, p))
    st = pl.BlockSpec((2, HEAD, HEAD, nb), lambda p: (p, 0, 0, 0))
    return pl.pallas_call(
        functools.partial(_wkv_lanes_kernel, t=t, nb=nb),
        grid=(c // LANES,),
        in_specs=[seq] * 6 + [st],
        out_specs=[seq, st],
        out_shape=[jax.ShapeDtypeStruct((rows, c), F32), jax.ShapeDtypeStruct(s0_t.shape, F32)],
        scratch_shapes=[pltpu.VMEM((6, t, LANES, nb), F32), pltpu.VMEM((t, LANES, nb), F32)],
        compiler_params=_params(("parallel",)),
        name="wkv_lanes",
    )(r, lw, k, v, a, b, s0_t)


def _chunk_masks():
    n = 2 * CHUNK
    ri = jnp.arange(n)[:, None]
    ci = jnp.arange(n)[None, :]
    levels = [(ri // 2) == (ci // 2)]
    bsz = 2
    while bsz < CHUNK:
        levels.append(((ri // (2 * bsz)) == (ci // (2 * bsz))) & ((ri // bsz) != (ci // bsz)))
        bsz *= 2
    levels.append(ri == ci)
    lv = jnp.stack(levels).astype(F32)
    r4 = jnp.arange(2 * n)[:, None]
    c4 = jnp.arange(2 * n)[None, :]
    same_head = ((r4 // CHUNK) % 2) == ((c4 // CHUNK) % 2)
    t, s = r4 % CHUNK, c4 % CHUNK
    m1 = jnp.where(r4 < n, same_head & (s < t), same_head & (s <= t)).astype(F32)
    tri = (jnp.arange(CHUNK)[:, None] >= jnp.arange(CHUNK)[None, :]).astype(BF16)
    return lv, m1, tri


def _post_math(y, bonus, v, gate, gn_g, gn_b, bd):
    ym = _head_sum(y, bd) * (1.0 / HEAD)
    yc = y - ym
    yv = _head_sum(yc * yc, bd) * (1.0 / HEAD)
    yn = yc * lax.rsqrt(yv + GN_EPS) * gn_g + gn_b
    return (yn + bonus * v) * gate


def _rwkv_chunk_kernel(pr_r, pr_k, pr_v, pr_l, hr_ref, hk_ref, hv_ref, hl_ref, mu_r, mu_k, mu_v, mu_l,
                       w0_ref, a0_ref, wdw_ref, wda_ref, g2_ref, kk_ref, ka_ref, rk_ref, gg_ref, gb_ref,
                       s0_ref, lv_ref, m1_ref, tri_ref, bd_ref, y_ref, st_ref, *, g, dr):
    ci = pl.program_id(2)
    n = 2 * CHUNK
    lane = lax.broadcasted_iota(jnp.int32, (1, LANES), 1)
    m0 = jnp.where(lane < HEAD, 1.0, 0.0).astype(F32)
    m1 = 1.0 - m0
    expand = lambda x: jnp.concatenate([x * m0, x * m1], axis=0)
    fold = lambda x: x[0:CHUNK] + x[CHUNK:n]

    @pl.when(ci == 0)
    def _():
        st_ref[...] = s0_ref[...]

    def shifted_lerp(x_ref, h_ref, mu_ref):
        x = x_ref[...]
        prev = h_ref[SUBLANES - 1:SUBLANES, :]
        prev = jnp.where(ci == 0, jnp.zeros_like(prev), prev)
        row = lax.broadcasted_iota(jnp.int32, x.shape, 0)
        shifted = jnp.where(row == 0, jnp.broadcast_to(prev, x.shape), pltpu.roll(x, 1, 0))
        return x + (shifted - x) * mu_ref[...]

    bd = bd_ref[...]
    r_all, lw_all, k_all, v_all, a_all, b_all, gate, bonus = _rwkv_vector_math(
        shifted_lerp(pr_r, hr_ref, mu_r), shifted_lerp(pr_k, hk_ref, mu_k), shifted_lerp(pr_v, hv_ref, mu_v),
        shifted_lerp(pr_l, hl_ref, mu_l), w0_ref[...], a0_ref[...], wdw_ref[...], wda_ref[...], g2_ref[...],
        kk_ref[...], ka_ref[...], rk_ref[...], bd, dr)

    pairs = range(g)
    tri = tri_ref[...]
    mask1 = m1_ref[...]
    nlev = lv_ref.shape[0]
    lv = [lv_ref[i] for i in range(nlev)]
    sls = [slice(p * LANES, (p + 1) * LANES) for p in pairs]
    r, lw, k, v, a, b = ([x[:, sl] for sl in sls] for x in (r_all, lw_all, k_all, v_all, a_all, b_all))
    s_old = [expand(st_ref[0, p]) for p in pairs]
    lw_split = [_split(x) for x in lw]
    cum = [_dot(tri, hi) + _dot(tri, lo) for hi, lo in lw_split]
    cum_l = [x[CHUNK - 1:CHUNK] for x in cum]
    e_neg = [jnp.exp(-x) for x in cum]
    e_hat = [jnp.exp(cl - x) for cl, x in zip(cum_l, cum)]
    at_e = [expand(a[p] * jnp.exp(cum[p] - lw[p])) for p in pairs]
    rt_e = [expand(r[p] * jnp.exp(cum[p])) for p in pairs]
    v_e = [expand(x) for x in v]
    bt = [(b[p] * e_neg[p]).astype(BF16) for p in pairs]
    kt = [(k[p] * e_neg[p]).astype(BF16) for p in pairs]
    nt = (((1,), (1,)), ((), ()))
    out1 = [lax.dot_general(jnp.concatenate([at_e[p], rt_e[p]], axis=0).astype(BF16),
                            jnp.concatenate([bt[p], bt[p], kt[p], kt[p]], axis=0), nt,
                            preferred_element_type=F32) for p in pairs]
    out1 = [jnp.where(mask1 != 0.0, x, 0.0) for x in out1]
    a_ab = [x[0:n, 0:n] for x in out1]
    m_r = [x[n:2 * n, :].astype(BF16) for x in out1]
    akv = [_dot(out1[p][0:n, n:2 * n].astype(BF16), v_e[p].astype(BF16)) for p in pairs]
    tm = [lv[nlev - 1] + x * lv[0] for x in a_ab]
    for lev in range(1, nlev - 1):
        mm = _mm_split_lhs if lev > SPLIT_FROM_LEVEL else (lambda x, y: _dot(x.astype(BF16), y.astype(BF16)))
        step = [mm(tm[p], a_ab[p] * lv[lev]) for p in pairs]
        tm = [tm[p] + mm(step[p], tm[p]) for p in pairs]
    wu = [_mm_split_both(tm[p], jnp.concatenate([at_e[p], akv[p]], axis=1)) for p in pairs]
    rhs4 = [jnp.concatenate([wu[p], jnp.concatenate([jnp.zeros_like(v_e[p]), v_e[p]], axis=1)],
                            axis=0).astype(BF16) for p in pairs]
    o4 = [_dot(m_r[p], rhs4[p]) for p in pairs]
    q = [fold(rt_e[p] + o4[p][:, 0:LANES]).astype(BF16) for p in pairs]
    y1 = [fold(o4[p][:, LANES:2 * LANES]) for p in pairs]
    rhs5 = [jnp.concatenate([expand(b[p] * e_hat[p]), expand(k[p] * e_hat[p])], axis=0).astype(BF16)
            for p in pairs]
    o5 = [lax.dot_general(rhs4[p], rhs5[p], (((0,), (0,)), ((), ())), preferred_element_type=F32)
          for p in pairs]
    s_bf = [x.astype(BF16) for x in s_old]
    y = [lax.dot_general(q[p], s_bf[p], nt, preferred_element_type=F32) + y1[p] for p in pairs]
    s_new = [s_old[p] * jnp.exp(cum_l[p]) + _dot(s_bf[p], o5[p][0:n].astype(BF16)) + o5[p][n:2 * n]
             for p in pairs]
    y_all = jnp.concatenate(y, axis=1)
    y_ref[...] = _post_math(y_all, bonus, v_all, gate, gg_ref[...], gb_ref[...], bd).astype(y_ref.dtype)
    for p in pairs:
        st_ref[0, p] = fold(s_new[p])


def _rwkv_chunk(pr, nseq, t, s0, mu, w0, a0, wda, g2, k_k, k_a, r_k, gn_g, gn_b, bd, c, dr):
    sw = pr.shape[1]
    npair = c // LANES
    nc = t // CHUNK
    g = _pick(npair, (8, 4, 2, 1))
    gw = g * LANES
    lw_ = sw - 3 * c
    assert (3 * c) % lw_ == 0 and c % gw == 0
    hb = CHUNK // SUBLANES
    lv, m1, tri = _chunk_masks()
    row_blk = lambda si, ci: si * nc + ci
    halo_blk = lambda si, ci: jnp.maximum((si * nc + ci) * hb - 1, 0)
    wide = lambda part: pl.BlockSpec((CHUNK, gw), lambda si, pi, ci: (row_blk(si, ci), part * (c // gw) + pi))
    wide_h = lambda part: pl.BlockSpec((SUBLANES, gw),
                                       lambda si, pi, ci: (halo_blk(si, ci), part * (c // gw) + pi))
    low = pl.BlockSpec((CHUNK, lw_), lambda si, pi, ci: (row_blk(si, ci), 3 * c // lw_))
    low_h = pl.BlockSpec((SUBLANES, lw_), lambda si, pi, ci: (halo_blk(si, ci), 3 * c // lw_))
    vec = lambda part: pl.BlockSpec((1, gw), lambda si, pi, ci: (0, part * (c // gw) + pi))
    vec_low = pl.BlockSpec((1, lw_), lambda si, pi, ci: (0, 3 * c // lw_))
    cols = lambda rows, part: pl.BlockSpec((rows, gw), lambda si, pi, ci: (0, part * (c // gw) + pi))
    st = pl.BlockSpec((1, g, HEAD, LANES), lambda si, pi, ci: (si, pi, 0, 0))
    const = lambda x: pl.BlockSpec(x.shape, lambda si, pi, ci: (0,) * x.ndim)
    return pl.pallas_call(
        functools.partial(_rwkv_chunk_kernel, g=g, dr=dr),
        grid=(nseq, npair // g, nc),
        in_specs=[wide(0), wide(1), wide(2), low, wide_h(0), wide_h(1), wide_h(2), low_h,
                  vec(0), vec(1), vec(2), vec_low,
                  vec(0), vec(0), cols(2 * dr, 0), cols(2 * dr, 1), cols(g2.shape[0], 0),
                  vec(0), vec(0), vec(0), vec(0), vec(0),
                  st, const(lv), const(m1), const(tri), const(bd)],
        out_specs=[pl.BlockSpec((CHUNK, gw), lambda si, pi, ci: (row_blk(si, ci), pi)), st],
        out_shape=[jax.ShapeDtypeStruct((nseq * t, c), BF16),
                   jax.ShapeDtypeStruct((nseq, npair, HEAD, LANES), F32)],
        compiler_params=_params(("parallel", "parallel", "arbitrary")),
        name="rwkv_chunk",
    )(pr, pr, pr, pr, pr, pr, pr, pr, mu, mu, mu, mu, w0, a0, wda, wda, g2, k_k, k_a, r_k, gn_g, gn_b,
      s0, lv, m1, tri, bd)


def _rwkv_post_kernel(y_ref, bn_ref, v_ref, g_ref, gg_ref, gb_ref, bd_ref, o_ref):
    o_ref[...] = _post_math(y_ref[...], bn_ref[...], v_ref[...], g_ref[...], gg_ref[...], gb_ref[...],
                            bd_ref[...]).astype(o_ref.dtype)


def _rwkv_post(y, bn, v, g, gn_g, gn_b, bd):
    m, c = y.shape
    tm = _pick(m, (256, 128, 64, 32, 16, 8))
    row = pl.BlockSpec((tm, c), lambda i: (i, 0))
    vec = pl.BlockSpec((1, c), lambda i: (0, 0))
    return pl.pallas_call(
        _rwkv_post_kernel,
        grid=(m // tm,),
        in_specs=[row, row, row, row, vec, vec, pl.BlockSpec((LANES, LANES), lambda i: (0, 0))],
        out_specs=row,
        out_shape=jax.ShapeDtypeStruct((m, c), BF16),
        compiler_params=_params(("parallel",)),
        name="rwkv_post",
    )(y, bn, v, g, gn_g, gn_b, bd)


def _unpack_state(s):
    n, hp = s.shape[0], s.shape[1]
    return s.reshape(n, hp, HEAD, 2, HEAD).transpose(0, 1, 3, 2, 4).reshape(n, 2 * hp, HEAD, HEAD)


def _layer(x_prompt, x_sample, wkv0, conv0, shift0,
           ln_mix_pre, ln_mix_post, ln_ffn_pre, ln_ffn_post, w_in, b_gate,
           conv_w, conv_b, conv_ln_g, conv_ln_b, w_conv_out, shift_mu,
           w0, w2, a0, a2, g2, k_k, k_a, r_k, gn_g, gn_b, w_rwkv_out, w_o,
           w_ffn_gate, w_ffn_up, w_ffn_down):
    bp, tp, d = x_prompt.shape
    bs, ts, _ = x_sample.shape
    c = conv_w.shape[1]
    kw = conv_w.shape[0]
    dr = w2.shape[0]
    sw = shift_mu.shape[0]
    mp, ms = bp * tp, bs * ts
    assert tp % CHUNK == 0 and tp >= kw - 1, "prompt sequences are processed in 64-token chunks"
    tm = _pick(math.gcd(mp, ms), (512, 256, 128, 64, 32, 16))
    row = lambda x: x.reshape(1, -1)

    tn_merge = _pick(math.gcd(d, 2 * c + sw), (256, 128))
    w_gates = _cast_tiles(w_in, 2 * c + sw, 2 * d, tn_merge)
    zeros = jnp.zeros((dr, c), F32)
    wda = jnp.concatenate([jnp.concatenate([w2, zeros], axis=1),
                           jnp.concatenate([zeros, a2], axis=1)], axis=0).astype(BF16)
    idx = jnp.arange(LANES) // HEAD
    bd = (idx[:, None] == idx[None, :]).astype(BF16)

    x_p = x_prompt.reshape(mp, d)
    x_s = x_sample.reshape(ms, d)
    h = _rms_cast(x_p, x_s, row(ln_mix_pre), min(tm, 256))

    conv_args = (conv_w, row(conv_b), row(conv_ln_g), row(conv_ln_b))
    u = _mm_pair(_mm_glu_kernel, h, w_in, 0, w_in, c, c, F32, "mm_glu")
    u_s = u[mp:].reshape(bs, ts, c)
    zc_p = _conv_seq(u, bp, tp, *conv_args)
    ext_s = jnp.concatenate([conv0, u_s], axis=1)
    zc_s = _conv_step(ext_s.transpose(1, 0, 2), *conv_args).transpose(1, 0, 2).reshape(ms, c)
    conv_p = jnp.stack([u[(q + 1) * tp - (kw - 1):(q + 1) * tp] for q in range(bp)])
    conv_s = ext_s[:, ts:]

    pr = _mm(h, w_in, sw, 2 * c, name="mm_shift")
    pr_s = pr[mp:].reshape(bs, ts, sw)
    shifted_s = jnp.concatenate([shift0[:, None], pr_s[:, :-1]], axis=1).reshape(ms, sw)
    g2_bf = g2.astype(BF16)
    consts = (row(shift_mu), row(w0), row(a0), wda, g2_bf, row(k_k), row(k_a), row(r_k), bd)
    yr_p, st_p = _rwkv_chunk(pr, bp, tp, jnp.zeros((bp, c // LANES, HEAD, LANES), F32), row(shift_mu),
                             row(w0), row(a0), wda, g2_bf, row(k_k), row(k_a), row(r_k),
                             row(gn_g), row(gn_b), bd, c, dr)
    if bs % LANES == 0:
        pr_tm = pr_s.transpose(1, 0, 2)
        shifted_tm = jnp.concatenate([shift0[None], pr_tm[:-1]], axis=0)
        vec_s = _rwkv_prep(pr_tm.reshape(ms, sw), shifted_tm.reshape(ms, sw), consts, c, dr)
        y_s, st_t = _wkv_lanes(*vec_s[:6], wkv0.transpose(1, 2, 3, 0), ts)
        yr_tm = _rwkv_post(y_s, vec_s[7], vec_s[3], vec_s[6], row(gn_g), row(gn_b), bd)
        yr_s = yr_tm.reshape(ts, bs, c).transpose(1, 0, 2).reshape(ms, c)
        st_s = st_t.transpose(3, 0, 1, 2)
    else:
        vec_s = _rwkv_prep(pr_s.reshape(ms, sw), shifted_s, consts, c, dr)
        y_s, st_s = _wkv_step(*vec_s[:6], wkv0, bd, ts)
        yr_s = _rwkv_post(y_s, vec_s[7], vec_s[3], vec_s[6], row(gn_g), row(gn_b), bd)
    shift_p = jnp.concatenate([pr[(q + 1) * tp - 1:(q + 1) * tp] for q in range(bp)], axis=0)
    shift_s = pr_s[:, -1]

    mixed = _merge(h, zc_p, zc_s, yr_p, yr_s, w_gates, _cast_tiles(w_conv_out, 0, d, tn_merge),
                   _cast_tiles(w_rwkv_out, 0, d, tn_merge), b_gate[0:1], b_gate[1:2], tm)
    o = _mm(mixed, w_o, out_dtype=BF16, name="mm_o")
    x1, h2 = _rms_res(x_p, x_s, o, row(ln_mix_post), row(ln_ffn_pre), min(tm, 256))

    act = _mm_pair(_mm_swiglu_kernel, h2, w_ffn_gate, 0, w_ffn_up, 0, w_ffn_gate.shape[1], BF16, "mm_swiglu")
    f = _mm_tiled(act, _cast_tiles(w_ffn_down, 0, d, _pick(d, (512, 256, 128))), BF16, "mm_down")
    tl = min(tm, 256)
    y_prompt = _rms_res_last(x1, f, row(ln_ffn_post), 0, mp, tl)
    y_sample = _rms_res_last(x1, f, row(ln_ffn_post), mp, ms, tl)

    return (y_prompt.reshape(bp, tp, d), y_sample.reshape(bs, ts, d),
            _unpack_state(st_p), conv_p, shift_p,
            st_s, conv_s, shift_s)


def kernel(x_prompt, x_sample, state_wkv, state_conv, state_shift, ln_mix_pre, ln_mix_post, ln_ffn_pre,
           ln_ffn_post, w_in, b_gate, conv_w, conv_b, conv_ln_g, conv_ln_b, w_conv_out, shift_mu, w0, w2,
           a0, a2, g2, k_k, k_a, r_k, gn_g, gn_b, w_rwkv_out, w_o, w_ffn_gate, w_ffn_up, w_ffn_down):
    depth = w_in.shape[0]
    assert depth == 1, "one decoder layer per step"
    weights = (ln_mix_pre, ln_mix_post, ln_ffn_pre, ln_ffn_post, w_in, b_gate,
               conv_w, conv_b, conv_ln_g, conv_ln_b, w_conv_out, shift_mu,
               w0, w2, a0, a2, g2, k_k, k_a, r_k, gn_g, gn_b, w_rwkv_out, w_o,
               w_ffn_gate, w_ffn_up, w_ffn_down)
    lw = tuple(wt[0] for wt in weights)
    yp, ys, wkv_p, conv_p, shift_p, wkv_s, conv_s, shift_s = _layer(
        x_prompt, x_sample, state_wkv[0], state_conv[0], state_shift[0], *lw)
    return (yp, ys, wkv_p[None], conv_p[None], shift_p[None],
            wkv_s[None], conv_s[None], shift_s[None])
```

```python
import functools
import math

import jax
import jax.numpy as jnp
from jax import lax
from jax.experimental import pallas as pl
from jax.experimental.pallas import tpu as pltpu

F32 = jnp.float32
BF16 = jnp.bfloat16

RMS_EPS = 1e-6
LN_EPS = 1e-5
GN_EPS = 64e-5
HEAD = 64
LANES = 128
SUBLANES = 8
CONV_HALO = 32
CHUNK = 64
SPLIT_FROM_LEVEL = 3
VMEM_LIMIT = 56 * 1024 * 1024


def _pick(n, cands):
    for c in cands:
        if n % c == 0:
            return c
    return n


def _params(sem):
    return pltpu.CompilerParams(dimension_semantics=sem, vmem_limit_bytes=VMEM_LIMIT)


def _dot(a, b):
    return jnp.dot(a, b, preferred_element_type=F32)


def _split(x):
    hi = x.astype(BF16)
    return hi, (x - hi.astype(F32)).astype(BF16)


def _mm_split_lhs(a, b):
    hi, lo = _split(a)
    bb = b.astype(BF16)
    return _dot(hi, bb) + _dot(lo, bb)


def _mm_split_both(a, b):
    ah, al = _split(a)
    bh, bl = _split(b)
    return _dot(ah, bh) + _dot(al, bh) + _dot(ah, bl)


def _head_sum(x, bd):
    cols = [_dot(x[:, c:c + LANES].astype(BF16), bd) for c in range(0, x.shape[1], LANES)]
    return jnp.concatenate(cols, axis=1)


def _two_group_specs(tm, d, nblk_p):
    return (pl.BlockSpec((tm, d), lambda i, *_: (jnp.minimum(i, nblk_p - 1), 0)),
            pl.BlockSpec((tm, d), lambda i, *_: (jnp.maximum(i - nblk_p, 0), 0)))


def _two_group_rows(p_ref, s_ref, nblk_p):
    return jnp.where(pl.program_id(0) < nblk_p, p_ref[...], s_ref[...])


def _rms_cast_kernel(xp_ref, xs_ref, g_ref, o_ref, *, nblk_p):
    x = _two_group_rows(xp_ref, xs_ref, nblk_p)
    ms = jnp.mean(x * x, axis=-1, keepdims=True)
    o_ref[...] = (x * lax.rsqrt(ms + RMS_EPS) * g_ref[...]).astype(o_ref.dtype)


def _rms_cast(x_p, x_s, g, tm):
    (mp, d), ms = x_p.shape, x_s.shape[0]
    xp_spec, xs_spec = _two_group_specs(tm, d, mp // tm)
    return pl.pallas_call(
        functools.partial(_rms_cast_kernel, nblk_p=mp // tm),
        grid=((mp + ms) // tm,),
        in_specs=[xp_spec, xs_spec, pl.BlockSpec((1, d), lambda i: (0, 0))],
        out_specs=pl.BlockSpec((tm, d), lambda i: (i, 0)),
        out_shape=jax.ShapeDtypeStruct((mp + ms, d), BF16),
        compiler_params=_params(("parallel",)),
        name="rms_cast",
    )(x_p, x_s, g)


def _rms_res_kernel(xp_ref, xs_ref, o_ref, g_ref, g2_ref, x1_ref, h_ref, *, nblk_p):
    o = o_ref[...].astype(F32)
    ms = jnp.mean(o * o, axis=-1, keepdims=True)
    x1 = _two_group_rows(xp_ref, xs_ref, nblk_p) + o * lax.rsqrt(ms + RMS_EPS) * g_ref[...]
    x1_ref[...] = x1
    ms1 = jnp.mean(x1 * x1, axis=-1, keepdims=True)
    h_ref[...] = (x1 * lax.rsqrt(ms1 + RMS_EPS) * g2_ref[...]).astype(h_ref.dtype)


def _rms_res(x_p, x_s, o, g, g2, tm):
    (mp, d), m = x_p.shape, o.shape[0]
    xp_spec, xs_spec = _two_group_specs(tm, d, mp // tm)
    row = pl.BlockSpec((tm, d), lambda i: (i, 0))
    vec = pl.BlockSpec((1, d), lambda i: (0, 0))
    return pl.pallas_call(
        functools.partial(_rms_res_kernel, nblk_p=mp // tm),
        grid=(m // tm,),
        in_specs=[xp_spec, xs_spec, row, vec, vec],
        out_specs=[row, row],
        out_shape=[jax.ShapeDtypeStruct((m, d), F32), jax.ShapeDtypeStruct((m, d), BF16)],
        compiler_params=_params(("parallel",)),
        name="rms_res",
    )(x_p, x_s, o, g, g2)


def _rms_res_last_kernel(x_ref, o_ref, g_ref, x1_ref):
    o = o_ref[...].astype(F32)
    ms = jnp.mean(o * o, axis=-1, keepdims=True)
    x1_ref[...] = x_ref[...] + o * lax.rsqrt(ms + RMS_EPS) * g_ref[...]


def _rms_res_last(x, o, g, row0, rows, tm):
    d = x.shape[1]
    blk0 = row0 // tm
    src = pl.BlockSpec((tm, d), lambda i: (blk0 + i, 0))
    return pl.pallas_call(
        _rms_res_last_kernel,
        grid=(rows // tm,),
        in_specs=[src, src, pl.BlockSpec((1, d), lambda i: (0, 0))],
        out_specs=pl.BlockSpec((tm, d), lambda i: (i, 0)),
        out_shape=jax.ShapeDtypeStruct((rows, d), F32),
        compiler_params=_params(("parallel",)),
        name="rms_res_last",
    )(x, o, g)


MM_VMEM_BUDGET = 50 * 1024 * 1024


def _mm_tiles(m, k, n, col0s, w_bytes, n_w, out_bytes):
    span = math.gcd(n, *col0s)
    tns = [t for t in (512, 256, 128) if span % t == 0] or [span]
    for tm in (1088, 1024, 544, 512, 256, 128, 64, 32, 16, 8):
        for tn in tns:
            need = (2 * tm * k * 2
                    + n_w * k * tn * (2 * w_bytes + (2 if w_bytes > 2 else 0))
                    + tm * tn * (2 * out_bytes + 4 * n_w))
            if m % tm == 0 and need <= MM_VMEM_BUDGET:
                return tm, tn
    return m, tns[-1]


def _mm_kernel(x_ref, w_ref, o_ref):
    o_ref[...] = _dot(x_ref[...], w_ref[...].astype(BF16)).astype(o_ref.dtype)


def _mm(x, w, n=None, col0=0, out_dtype=F32, name="mm"):
    m, k = x.shape
    n = w.shape[1] if n is None else n
    tm, tn = _mm_tiles(m, k, n, (col0,), w.dtype.itemsize, 1, jnp.dtype(out_dtype).itemsize)
    j0 = col0 // tn
    return pl.pallas_call(
        _mm_kernel,
        grid=(m // tm, n // tn),
        in_specs=[pl.BlockSpec((tm, k), lambda i, j: (i, 0)),
                  pl.BlockSpec((k, tn), lambda i, j: (0, j0 + j))],
        out_specs=pl.BlockSpec((tm, tn), lambda i, j: (i, j)),
        out_shape=jax.ShapeDtypeStruct((m, n), out_dtype),
        compiler_params=_params(("parallel", "arbitrary")),
        name=name,
    )(x, w)


def _mm_glu_kernel(x_ref, wa_ref, wb_ref, o_ref):
    x = x_ref[...]
    o_ref[...] = _dot(x, wa_ref[...].astype(BF16)) * jax.nn.sigmoid(_dot(x, wb_ref[...].astype(BF16)))


def _mm_swiglu_kernel(x_ref, wg_ref, wu_ref, o_ref):
    x = x_ref[...]
    o_ref[...] = (jax.nn.silu(_dot(x, wg_ref[...].astype(BF16)))
                  * _dot(x, wu_ref[...].astype(BF16))).astype(o_ref.dtype)


def _mm_pair(kern, x, wa, col_a, wb, col_b, n, out_dtype, name):
    m, k = x.shape
    tm, tn = _mm_tiles(m, k, n, (col_a, col_b), wa.dtype.itemsize, 2, jnp.dtype(out_dtype).itemsize)
    ja, jb = col_a // tn, col_b // tn
    return pl.pallas_call(
        kern,
        grid=(m // tm, n // tn),
        in_specs=[pl.BlockSpec((tm, k), lambda i, j: (i, 0)),
                  pl.BlockSpec((k, tn), lambda i, j: (0, ja + j)),
                  pl.BlockSpec((k, tn), lambda i, j: (0, jb + j))],
        out_specs=pl.BlockSpec((tm, tn), lambda i, j: (i, j)),
        out_shape=jax.ShapeDtypeStruct((m, n), out_dtype),
        compiler_params=_params(("parallel", "arbitrary")),
        name=name,
    )(x, wa, wb)


def _merge_kernel(h_ref, zcp_ref, zcs_ref, yrp_ref, yrs_ref, wg0_ref, wg1_ref, wc_ref, wr_ref,
                  b0_ref, b1_ref, o_ref, *, nblk_p):
    h = h_ref[...]
    g0 = jax.nn.sigmoid(_dot(h, wg0_ref[0]) + b0_ref[...])
    g1 = jax.nn.sigmoid(_dot(h, wg1_ref[0]) + b1_ref[...])
    oc = _dot(_two_group_rows(zcp_ref, zcs_ref, nblk_p), wc_ref[0])
    orr = _dot(_two_group_rows(yrp_ref, yrs_ref, nblk_p), wr_ref[0])
    o_ref[...] = (g0 * oc + g1 * orr).astype(o_ref.dtype)


def _cast_kernel(*refs):
    *w_refs, o_ref = refs
    sb = w_refs[0].shape[1]
    for q, w_ref in enumerate(w_refs):
        o_ref[0, :, q * sb:(q + 1) * sb] = w_ref[...].astype(o_ref.dtype)


def _cast_tiles(w, col0, n, tn):
    k = w.shape[0]
    sb = math.gcd(col0, tn)
    parts = tn // sb
    assert n % tn == 0 and sb % LANES == 0
    b0 = col0 // sb
    tk = _pick(k, (4096, 2752, 2048, 1024, 512, 256, 128, 64, 32, 16))
    return pl.pallas_call(
        _cast_kernel,
        grid=(n // tn, k // tk),
        in_specs=[pl.BlockSpec((tk, sb), lambda j, kb, q=q: (kb, b0 + parts * j + q)) for q in range(parts)],
        out_specs=pl.BlockSpec((1, tk, tn), lambda j, kb: (j, kb, 0)),
        out_shape=jax.ShapeDtypeStruct((n // tn, k, tn), BF16),
        compiler_params=_params(("parallel", "parallel")),
        name="cast_tiles",
    )(*([w] * parts))


def _mm_tiled_kernel(x_ref, w_ref, o_ref):
    o_ref[...] = _dot(x_ref[...], w_ref[0]).astype(o_ref.dtype)


def _mm_tiled(x, wt, out_dtype, name):
    m, k = x.shape
    nt, _, tn = wt.shape
    out_bytes = jnp.dtype(out_dtype).itemsize
    tm = next(t for t in (1088, 1024, 544, 512, 256, 128, 64, 32, 16, 8, m) if m % t == 0
              and 2 * t * k * 2 + 2 * k * tn * 2 + t * tn * (2 * out_bytes + 4) <= MM_VMEM_BUDGET)
    return pl.pallas_call(
        _mm_tiled_kernel,
        grid=(m // tm, nt),
        in_specs=[pl.BlockSpec((tm, k), lambda i, j: (i, 0)),
                  pl.BlockSpec((1, k, tn), lambda i, j: (j, 0, 0))],
        out_specs=pl.BlockSpec((tm, tn), lambda i, j: (i, j)),
        out_shape=jax.ShapeDtypeStruct((m, nt * tn), out_dtype),
        compiler_params=_params(("parallel", "arbitrary")),
        name=name,
    )(x, wt)


def _merge(h, zc_p, zc_s, yr_p, yr_s, wg, wc, wr, b0, b1, tm):
    m, d = h.shape
    mp, c = zc_p.shape
    tn = wg.shape[2]
    gp_spec, gs_spec = _two_group_specs(tm, c, mp // tm)
    return pl.pallas_call(
        functools.partial(_merge_kernel, nblk_p=mp // tm),
        grid=(m // tm, d // tn),
        in_specs=[pl.BlockSpec((tm, d), lambda i, j: (i, 0)),
                  gp_spec, gs_spec, gp_spec, gs_spec,
                  pl.BlockSpec((1, d, tn), lambda i, j: (j, 0, 0)),
                  pl.BlockSpec((1, d, tn), lambda i, j: (d // tn + j, 0, 0)),
                  pl.BlockSpec((1, c, tn), lambda i, j: (j, 0, 0)),
                  pl.BlockSpec((1, c, tn), lambda i, j: (j, 0, 0)),
                  pl.BlockSpec((1, tn), lambda i, j: (0, j)),
                  pl.BlockSpec((1, tn), lambda i, j: (0, j))],
        out_specs=pl.BlockSpec((tm, tn), lambda i, j: (i, j)),
        out_shape=jax.ShapeDtypeStruct((m, d), BF16),
        compiler_params=_params(("parallel", "arbitrary")),
        name="merge",
    )(h, zc_p, zc_s, yr_p, yr_s, wg, wg, wc, wr, b0, b1)


def _ln_silu(z, g, b):
    mu = jnp.mean(z, axis=-1, keepdims=True)
    zc = z - mu
    var = jnp.mean(zc * zc, axis=-1, keepdims=True)
    y = zc * lax.rsqrt(var + LN_EPS) * g + b
    return y * jax.nn.sigmoid(y)


def _conv_seq_kernel(um_ref, uh_ref, w_ref, cb_ref, g_ref, b_ref, o_ref, x_scr, xs_scr, z_scr,
                     *, tb, kw, lc):
    i = pl.program_id(1)
    c = um_ref.shape[1]
    halo = uh_ref[...]
    x_scr[0:CONV_HALO, :] = jnp.where(i == 0, jnp.zeros_like(halo), halo)
    x_scr[CONV_HALO:CONV_HALO + tb, :] = um_ref[...]
    off = CONV_HALO - (kw - 1)
    first = True
    for s in range(SUBLANES):
        taps = [j for j in range(kw) if (off + j) % SUBLANES == s]
        if not taps:
            continue
        span = max(off + j - s for j in taps) + tb
        xs_scr[0:span, :] = x_scr[s:s + span, :]
        for c0 in range(0, c, lc):
            cols = slice(c0, c0 + lc)
            acc = cb_ref[:, cols] if first else z_scr[:, cols]
            for j in taps:
                a0 = off + j - s
                acc = acc + xs_scr[a0:a0 + tb, cols] * w_ref[j:j + 1, cols]
            z_scr[:, cols] = acc
        first = False
    o_ref[...] = _ln_silu(z_scr[...], g_ref[...], b_ref[...]).astype(o_ref.dtype)


def _conv_seq(u, nseq, t, conv_w, conv_b, ln_g, ln_b):
    c = u.shape[1]
    kw = conv_w.shape[0]
    tb = _pick(t, (128, 64, 32))
    lc = _pick(c, (128,))
    nb = t // tb
    hb = tb // CONV_HALO
    vec = pl.BlockSpec((1, c), lambda bi, i: (0, 0))
    return pl.pallas_call(
        functools.partial(_conv_seq_kernel, tb=tb, kw=kw, lc=lc),
        grid=(nseq, nb),
        in_specs=[pl.BlockSpec((tb, c), lambda bi, i: (bi * nb + i, 0)),
                  pl.BlockSpec((CONV_HALO, c), lambda bi, i: (jnp.maximum((bi * nb + i) * hb - 1, 0), 0)),
                  pl.BlockSpec((kw, c), lambda bi, i: (0, 0)),
                  vec, vec, vec],
        out_specs=pl.BlockSpec((tb, c), lambda bi, i: (bi * nb + i, 0)),
        out_shape=jax.ShapeDtypeStruct((nseq * t, c), BF16),
        scratch_shapes=[pltpu.VMEM((CONV_HALO + tb, c), F32), pltpu.VMEM((CONV_HALO + tb, c), F32),
                        pltpu.VMEM((tb, c), F32)],
        compiler_params=_params(("parallel", "arbitrary")),
        name="conv_seq",
    )(u, u, conv_w, conv_b, ln_g, ln_b)


def _conv_step_kernel(e_ref, w_ref, cb_ref, g_ref, b_ref, o_ref, *, kw, nt):
    for t in range(nt):
        acc = e_ref[t] * w_ref[0:1, :]
        for j in range(1, kw):
            acc = acc + e_ref[t + j] * w_ref[j:j + 1, :]
        o_ref[t] = _ln_silu(acc + cb_ref[...], g_ref[...], b_ref[...]).astype(o_ref.dtype)


def _conv_step(ext_tm, conv_w, conv_b, ln_g, ln_b):
    te, b, c = ext_tm.shape
    kw = conv_w.shape[0]
    nt = te - (kw - 1)
    sb = _pick(b, (8,))
    vec = pl.BlockSpec((1, c), lambda i: (0, 0))
    return pl.pallas_call(
        functools.partial(_conv_step_kernel, kw=kw, nt=nt),
        grid=(b // sb,),
        in_specs=[pl.BlockSpec((te, sb, c), lambda i: (0, i, 0)),
                  pl.BlockSpec((kw, c), lambda i: (0, 0)),
                  vec, vec, vec],
        out_specs=pl.BlockSpec((nt, sb, c), lambda i: (0, i, 0)),
        out_shape=jax.ShapeDtypeStruct((nt, b, c), BF16),
        compiler_params=_params(("parallel",)),
        name="conv_step",
    )(ext_tm, conv_w, conv_b, ln_g, ln_b)


def _rwkv_vector_math(m_r, m_k, m_v, m_low, w0, a0, wd_w, wd_a, g2, k_k, k_a, r_k, bd, dr):
    low = m_low[:, 0:2 * dr]
    lane = lax.broadcasted_iota(jnp.int32, low.shape, 1)
    low = jnp.where(lane < dr, jnp.tanh(low), low).astype(BF16)
    logw = -math.exp(-0.5) * jax.nn.sigmoid(w0 + _dot(low, wd_w))
    rate = jax.nn.sigmoid(a0 + _dot(low, wd_a))
    gate = _dot(jax.nn.sigmoid(m_low[:, 2 * dr:]).astype(BF16), g2)
    kk = m_k * k_k
    kk = kk / jnp.maximum(jnp.sqrt(_head_sum(kk * kk, bd)), 1e-12)
    k2 = m_k * (1.0 + (rate - 1.0) * k_a)
    bonus = _head_sum(m_r * k2 * r_k, bd)
    return m_r, logw, k2, m_v, -kk, kk * rate, gate, bonus


def _rwkv_prep_kernel(pr_ref, sh_ref, mu_ref, w0_ref, a0_ref, wda_ref, g2_ref, kk_ref, ka_ref, rk_ref,
                      bd_ref, *outs, c, dr):
    pr = pr_ref[...]
    m = pr + (sh_ref[...] - pr) * mu_ref[...]
    vals = _rwkv_vector_math(m[:, 0:c], m[:, c:2 * c], m[:, 2 * c:3 * c], m[:, 3 * c:],
                             w0_ref[...], a0_ref[...], wda_ref[:, 0:c], wda_ref[:, c:2 * c], g2_ref[...],
                             kk_ref[...], ka_ref[...], rk_ref[...], bd_ref[...], dr)
    for o_ref, val in zip(outs, vals):
        o_ref[...] = val


def _rwkv_prep(pr, shifted, consts, c, dr):
    rows, sw = pr.shape
    tm = _pick(rows, (128, 64, 32, 16, 8))
    row = pl.BlockSpec((tm, sw), lambda i: (i, 0))
    out = pl.BlockSpec((tm, c), lambda i: (i, 0))
    full = lambda x: pl.BlockSpec(x.shape, lambda i: (0,) * x.ndim)
    return pl.pallas_call(
        functools.partial(_rwkv_prep_kernel, c=c, dr=dr),
        grid=(rows // tm,),
        in_specs=[row, row] + [full(x) for x in consts],
        out_specs=[out] * 8,
        out_shape=[jax.ShapeDtypeStruct((rows, c), F32)] * 8,
        compiler_params=_params(("parallel",)),
        name="rwkv_prep",
    )(pr, shifted, *consts)


def _wkv_step_kernel(r_ref, w_ref, k_ref, v_ref, a_ref, b_ref, s0_ref, bd_ref, y_ref, st_ref,
                     *, nseq, t, g):
    bd = bd_ref[...]
    rows = lax.broadcasted_iota(jnp.int32, (HEAD, LANES), 0)
    lanes = lax.broadcasted_iota(jnp.int32, (HEAD, LANES), 1)
    diag = jnp.where(rows == lanes % HEAD, 1.0, 0.0).astype(F32)
    chains = [(q, p) for q in range(nseq) for p in range(g)]
    rr, kk, vv, aa, bb = (ref[...] for ref in (r_ref, k_ref, v_ref, a_ref, b_ref))
    ww = jnp.exp(w_ref[...])
    s = [jnp.concatenate([s0_ref[q, 2 * p], s0_ref[q, 2 * p + 1]], axis=1) for q, p in chains]
    yrows = {}
    for i in range(t):
        bc = lambda x, q, p: jnp.broadcast_to(
            x[q * t + i:q * t + i + 1, p * LANES:(p + 1) * LANES], (HEAD, LANES))
        lhs = jnp.concatenate(
            [jnp.concatenate([s[n] * bc(aa, q, p), bc(vv, q, p) * diag], axis=0)
             for n, (q, p) in enumerate(chains)], axis=0)
        res = _dot(lhs.astype(BF16), bd)
        s = [s[n] * bc(ww, q, p) + res[2 * n * HEAD:(2 * n + 1) * HEAD] * bc(bb, q, p)
             + res[(2 * n + 1) * HEAD:(2 * n + 2) * HEAD] * bc(kk, q, p)
             for n, (q, p) in enumerate(chains)]
        sr = jnp.concatenate([s[n] * bc(rr, q, p) for n, (q, p) in enumerate(chains)], axis=0)
        yc = _dot(sr.astype(BF16), bd)
        for n, (q, p) in enumerate(chains):
            yrows[(q, i, p)] = jnp.sum(yc[n * HEAD:(n + 1) * HEAD] * diag, axis=0, keepdims=True)
    y_ref[...] = jnp.concatenate(
        [jnp.concatenate([yrows[(q, i, p)] for p in range(g)], axis=1)
         for q in range(nseq) for i in range(t)], axis=0)
    for n, (q, p) in enumerate(chains):
        st_ref[q, 2 * p] = s[n][:, 0:HEAD]
        st_ref[q, 2 * p + 1] = s[n][:, HEAD:2 * HEAD]


def _wkv_step(r, lw, k, v, a, b, s0, bd, t):
    rows, c = r.shape
    npair = c // LANES
    assert SUBLANES % t == 0, "token-by-token path expects a few new tokens per sequence"
    nseq = SUBLANES // t
    g = _pick(npair, (8, 4, 2, 1))
    seq = pl.BlockSpec((nseq * t, g * LANES), lambda si, pi: (si, pi))
    st = pl.BlockSpec((nseq, 2 * g, HEAD, HEAD), lambda si, pi: (si, pi, 0, 0))
    return pl.pallas_call(
        functools.partial(_wkv_step_kernel, nseq=nseq, t=t, g=g),
        grid=(rows // (nseq * t), npair // g),
        in_specs=[seq] * 6 + [st, pl.BlockSpec((LANES, LANES), lambda si, pi: (0, 0))],
        out_specs=[seq, st],
        out_shape=[jax.ShapeDtypeStruct((rows, c), F32),
                   jax.ShapeDtypeStruct(s0.shape, F32)],
        compiler_params=_params(("parallel", "parallel")),
        name="wkv_step",
    )(r, lw, k, v, a, b, s0, bd)


def _wkv_lanes_kernel(r_ref, w_ref, k_ref, v_ref, a_ref, b_ref, s0_ref, y_ref, st_ref, xt_scr, yt_scr,
                      *, t, nb):
    names = (r_ref, w_ref, k_ref, v_ref, a_ref, b_ref)
    for i, ref in enumerate(names):
        for step in range(t):
            x = ref[step * nb:(step + 1) * nb, :].T
            xt_scr[i, step] = jnp.exp(x) if ref is w_ref else x
    ir, iw, ik, iv, ia, ib = range(6)
    sub = lax.broadcasted_iota(jnp.int32, (SUBLANES, nb), 0)
    for hh in range(2):
        ch = slice(hh * HEAD, (hh + 1) * HEAD)

        def group(vg, carry, hh=hh, ch=ch):
            v0 = pl.multiple_of(vg * SUBLANES, SUBLANES)
            vt = [xt_scr[iv, step, pl.ds(hh * HEAD + v0, SUBLANES), :] for step in range(t)]
            ytile = [jnp.zeros((SUBLANES, nb), F32) for _ in range(t)]
            for vi in range(SUBLANES):
                s = s0_ref[hh, v0 + vi]
                for step in range(t):
                    sa = jnp.sum(s * xt_scr[ia, step, ch, :], axis=0, keepdims=True)
                    s = (s * xt_scr[iw, step, ch, :] + sa * xt_scr[ib, step, ch, :]
                         + vt[step][vi:vi + 1, :] * xt_scr[ik, step, ch, :])
                    yrow = jnp.sum(s * xt_scr[ir, step, ch, :], axis=0, keepdims=True)
                    ytile[step] = jnp.where(sub == vi, yrow, ytile[step])
                st_ref[hh, v0 + vi] = s
            for step in range(t):
                yt_scr[step, pl.ds(hh * HEAD + v0, SUBLANES), :] = ytile[step]
            return carry

        lax.fori_loop(0, HEAD // SUBLANES, group, 0)
    for step in range(t):
        y_ref[step * nb:(step + 1) * nb, :] = yt_scr[step].T


def _wkv_lanes(r, lw, k, v, a, b, s0_t, t):
    rows, c = r.shape
    nb = rows // t
    seq = pl.BlockSpec((rows, LANES), lambda p: (0, p))
    st = pl.BlockSpec((2, HEAD, HEAD, nb), lambda p: (p, 0, 0, 0))
    return pl.pallas_call(
        functools.partial(_wkv_lanes_kernel, t=t, nb=nb),
        grid=(c // LANES,),
        in_specs=[seq] * 6 + [st],
        out_specs=[seq, st],
        out_shape=[jax.ShapeDtypeStruct((rows, c), F32), jax.ShapeDtypeStruct(s0_t.shape, F32)],
        scratch_shapes=[pltpu.VMEM((6, t, LANES, nb), F32), pltpu.VMEM((t, LANES, nb), F32)],
        compiler_params=_params(("parallel",)),
        name="wkv_lanes",
    )(r, lw, k, v, a, b, s0_t)


def _chunk_masks():
    n = 2 * CHUNK
    ri = jnp.arange(n)[:, None]
    ci = jnp.arange(n)[None, :]
    levels = [(ri // 2) == (ci // 2)]
    bsz = 2
    while bsz < CHUNK:
        levels.append(((ri // (2 * bsz)) == (ci // (2 * bsz))) & ((ri // bsz) != (ci // bsz)))
        bsz *= 2
    levels.append(ri == ci)
    lv = jnp.stack(levels).astype(F32)
    r4 = jnp.arange(2 * n)[:, None]
    c4 = jnp.arange(2 * n)[None, :]
    same_head = ((r4 // CHUNK) % 2) == ((c4 // CHUNK) % 2)
    t, s = r4 % CHUNK, c4 % CHUNK
    m1 = jnp.where(r4 < n, same_head & (s < t), same_head & (s <= t)).astype(F32)
    tri = (jnp.arange(CHUNK)[:, None] >= jnp.arange(CHUNK)[None, :]).astype(BF16)
    return lv, m1, tri


def _post_math(y, bonus, v, gate, gn_g, gn_b, bd):
    ym = _head_sum(y, bd) * (1.0 / HEAD)
    yc = y - ym
    yv = _head_sum(yc * yc, bd) * (1.0 / HEAD)
    yn = yc * lax.rsqrt(yv + GN_EPS) * gn_g + gn_b
    return (yn + bonus * v) * gate


def _rwkv_chunk_kernel(pr_r, pr_k, pr_v, pr_l, hr_ref, hk_ref, hv_ref, hl_ref, mu_r, mu_k, mu_v, mu_l,
                       w0_ref, a0_ref, wdw_ref, wda_ref, g2_ref, kk_ref, ka_ref, rk_ref, gg_ref, gb_ref,
                       s0_ref, lv_ref, m1_ref, tri_ref, bd_ref, y_ref, st_ref, *, g, dr):
    ci = pl.program_id(2)
    n = 2 * CHUNK
    lane = lax.broadcasted_iota(jnp.int32, (1, LANES), 1)
    m0 = jnp.where(lane < HEAD, 1.0, 0.0).astype(F32)
    m1 = 1.0 - m0
    expand = lambda x: jnp.concatenate([x * m0, x * m1], axis=0)
    fold = lambda x: x[0:CHUNK] + x[CHUNK:n]

    @pl.when(ci == 0)
    def _():
        st_ref[...] = s0_ref[...]

    def shifted_lerp(x_ref, h_ref, mu_ref):
        x = x_ref[...]
        prev = h_ref[SUBLANES - 1:SUBLANES, :]
        prev = jnp.where(ci == 0, jnp.zeros_like(prev), prev)
        row = lax.broadcasted_iota(jnp.int32, x.shape, 0)
        shifted = jnp.where(row == 0, jnp.broadcast_to(prev, x.shape), pltpu.roll(x, 1, 0))
        return x + (shifted - x) * mu_ref[...]

    bd = bd_ref[...]
    r_all, lw_all, k_all, v_all, a_all, b_all, gate, bonus = _rwkv_vector_math(
        shifted_lerp(pr_r, hr_ref, mu_r), shifted_lerp(pr_k, hk_ref, mu_k), shifted_lerp(pr_v, hv_ref, mu_v),
        shifted_lerp(pr_l, hl_ref, mu_l), w0_ref[...], a0_ref[...], wdw_ref[...], wda_ref[...], g2_ref[...],
        kk_ref[...], ka_ref[...], rk_ref[...], bd, dr)

    pairs = range(g)
    tri = tri_ref[...]
    mask1 = m1_ref[...]
    nlev = lv_ref.shape[0]
    lv = [lv_ref[i] for i in range(nlev)]
    sls = [slice(p * LANES, (p + 1) * LANES) for p in pairs]
    r, lw, k, v, a, b = ([x[:, sl] for sl in sls] for x in (r_all, lw_all, k_all, v_all, a_all, b_all))
    s_old = [expand(st_ref[0, p]) for p in pairs]
    lw_split = [_split(x) for x in lw]
    cum = [_dot(tri, hi) + _dot(tri, lo) for hi, lo in lw_split]
    cum_l = [x[CHUNK - 1:CHUNK] for x in cum]
    e_neg = [jnp.exp(-x) for x in cum]
    e_hat = [jnp.exp(cl - x) for cl, x in zip(cum_l, cum)]
    at_e = [expand(a[p] * jnp.exp(cum[p] - lw[p])) for p in pairs]
    rt_e = [expand(r[p] * jnp.exp(cum[p])) for p in pairs]
    v_e = [expand(x) for x in v]
    bt = [(b[p] * e_neg[p]).astype(BF16) for p in pairs]
    kt = [(k[p] * e_neg[p]).astype(BF16) for p in pairs]
    nt = (((1,), (1,)), ((), ()))
    out1 = [lax.dot_general(jnp.concatenate([at_e[p], rt_e[p]], axis=0).astype(BF16),
                            jnp.concatenate([bt[p], bt[p], kt[p], kt[p]], axis=0), nt,
                            preferred_element_type=F32) for p in pairs]
    out1 = [jnp.where(mask1 != 0.0, x, 0.0) for x in out1]
    a_ab = [x[0:n, 0:n] for x in out1]
    m_r = [x[n:2 * n, :].astype(BF16) for x in out1]
    akv = [_dot(out1[p][0:n, n:2 * n].astype(BF16), v_e[p].astype(BF16)) for p in pairs]
    tm = [lv[nlev - 1] + x * lv[0] for x in a_ab]
    for lev in range(1, nlev - 1):
        mm = _mm_split_lhs if lev > SPLIT_FROM_LEVEL else (lambda x, y: _dot(x.astype(BF16), y.astype(BF16)))
        step = [mm(tm[p], a_ab[p] * lv[lev]) for p in pairs]
        tm = [tm[p] + mm(step[p], tm[p]) for p in pairs]
    wu = [_mm_split_both(tm[p], jnp.concatenate([at_e[p], akv[p]], axis=1)) for p in pairs]
    rhs4 = [jnp.concatenate([wu[p], jnp.concatenate([jnp.zeros_like(v_e[p]), v_e[p]], axis=1)],
                            axis=0).astype(BF16) for p in pairs]
    o4 = [_dot(m_r[p], rhs4[p]) for p in pairs]
    q = [fold(rt_e[p] + o4[p][:, 0:LANES]).astype(BF16) for p in pairs]
    y1 = [fold(o4[p][:, LANES:2 * LANES]) for p in pairs]
    rhs5 = [jnp.concatenate([expand(b[p] * e_hat[p]), expand(k[p] * e_hat[p])], axis=0).astype(BF16)
            for p in pairs]
    o5 = [lax.dot_general(rhs4[p], rhs5[p], (((0,), (0,)), ((), ())), preferred_element_type=F32)
          for p in pairs]
    s_bf = [x.astype(BF16) for x in s_old]
    y = [lax.dot_general(q[p], s_bf[p], nt, preferred_element_type=F32) + y1[p] for p in pairs]
    s_new = [s_old[p] * jnp.exp(cum_l[p]) + _dot(s_bf[p], o5[p][0:n].astype(BF16)) + o5[p][n:2 * n]
             for p in pairs]
    y_all = jnp.concatenate(y, axis=1)
    y_ref[...] = _post_math(y_all, bonus, v_all, gate, gg_ref[...], gb_ref[...], bd).astype(y_ref.dtype)
    for p in pairs:
        st_ref[0, p] = fold(s_new[p])


def _rwkv_chunk(pr, pr_low, nseq, t, s0, mu, w0, a0, wda, g2, k_k, k_a, r_k, gn_g, gn_b, bd, c, dr):
    npair = c // LANES
    nc = t // CHUNK
    g = _pick(npair, (8, 4, 2, 1))
    gw = g * LANES
    lw_ = pr_low.shape[1]
    assert pr.shape[1] == 3 * c and (3 * c) % lw_ == 0 and c % gw == 0
    hb = CHUNK // SUBLANES
    lv, m1, tri = _chunk_masks()
    row_blk = lambda si, ci: si * nc + ci
    halo_blk = lambda si, ci: jnp.maximum((si * nc + ci) * hb - 1, 0)
    wide = lambda part: pl.BlockSpec((CHUNK, gw), lambda si, pi, ci: (row_blk(si, ci), part * (c // gw) + pi))
    wide_h = lambda part: pl.BlockSpec((SUBLANES, gw),
                                       lambda si, pi, ci: (halo_blk(si, ci), part * (c // gw) + pi))
    low = pl.BlockSpec((CHUNK, lw_), lambda si, pi, ci: (row_blk(si, ci), 0))
    low_h = pl.BlockSpec((SUBLANES, lw_), lambda si, pi, ci: (halo_blk(si, ci), 0))
    vec = lambda part: pl.BlockSpec((1, gw), lambda si, pi, ci: (0, part * (c // gw) + pi))
    vec_low = pl.BlockSpec((1, lw_), lambda si, pi, ci: (0, 3 * c // lw_))
    cols = lambda rows, part: pl.BlockSpec((rows, gw), lambda si, pi, ci: (0, part * (c // gw) + pi))
    st = pl.BlockSpec((1, g, HEAD, LANES), lambda si, pi, ci: (si, pi, 0, 0))
    const = lambda x: pl.BlockSpec(x.shape, lambda si, pi, ci: (0,) * x.ndim)
    return pl.pallas_call(
        functools.partial(_rwkv_chunk_kernel, g=g, dr=dr),
        grid=(nseq, npair // g, nc),
        in_specs=[wide(0), wide(1), wide(2), low, wide_h(0), wide_h(1), wide_h(2), low_h,
                  vec(0), vec(1), vec(2), vec_low,
                  vec(0), vec(0), cols(2 * dr, 0), cols(2 * dr, 1), cols(g2.shape[0], 0),
                  vec(0), vec(0), vec(0), vec(0), vec(0),
                  st, const(lv), const(m1), const(tri), const(bd)],
        out_specs=[pl.BlockSpec((CHUNK, gw), lambda si, pi, ci: (row_blk(si, ci), pi)), st],
        out_shape=[jax.ShapeDtypeStruct((nseq * t, c), BF16),
                   jax.ShapeDtypeStruct((nseq, npair, HEAD, LANES), F32)],
        compiler_params=_params(("parallel", "parallel", "arbitrary")),
        name="rwkv_chunk",
    )(pr, pr, pr, pr_low, pr, pr, pr, pr_low, mu, mu, mu, mu, w0, a0, wda, wda, g2, k_k, k_a, r_k, gn_g, gn_b,
      s0, lv, m1, tri, bd)


def _rwkv_post_kernel(y_ref, bn_ref, v_ref, g_ref, gg_ref, gb_ref, bd_ref, o_ref):
    o_ref[...] = _post_math(y_ref[...], bn_ref[...], v_ref[...], g_ref[...], gg_ref[...], gb_ref[...],
                            bd_ref[...]).astype(o_ref.dtype)


def _rwkv_post(y, bn, v, g, gn_g, gn_b, bd):
    m, c = y.shape
    tm = _pick(m, (256, 128, 64, 32, 16, 8))
    row = pl.BlockSpec((tm, c), lambda i: (i, 0))
    vec = pl.BlockSpec((1, c), lambda i: (0, 0))
    return pl.pallas_call(
        _rwkv_post_kernel,
        grid=(m // tm,),
        in_specs=[row, row, row, row, vec, vec, pl.BlockSpec((LANES, LANES), lambda i: (0, 0))],
        out_specs=row,
        out_shape=jax.ShapeDtypeStruct((m, c), BF16),
        compiler_params=_params(("parallel",)),
        name="rwkv_post",
    )(y, bn, v, g, gn_g, gn_b, bd)


def _unpack_state(s):
    n, hp = s.shape[0], s.shape[1]
    return s.reshape(n, hp, HEAD, 2, HEAD).transpose(0, 1, 3, 2, 4).reshape(n, 2 * hp, HEAD, HEAD)


def _layer(x_prompt, x_sample, wkv0, conv0, shift0,
           ln_mix_pre, ln_mix_post, ln_ffn_pre, ln_ffn_post, w_in, b_gate,
           conv_w, conv_b, conv_ln_g, conv_ln_b, w_conv_out, shift_mu,
           w0, w2, a0, a2, g2, k_k, k_a, r_k, gn_g, gn_b, w_rwkv_out, w_o,
           w_ffn_gate, w_ffn_up, w_ffn_down):
    bp, tp, d = x_prompt.shape
    bs, ts, _ = x_sample.shape
    c = conv_w.shape[1]
    kw = conv_w.shape[0]
    dr = w2.shape[0]
    sw = shift_mu.shape[0]
    mp, ms = bp * tp, bs * ts
    assert tp % CHUNK == 0 and tp >= kw - 1, "prompt sequences are processed in 64-token chunks"
    tm = _pick(math.gcd(mp, ms), (512, 256, 128, 64, 32, 16))
    row = lambda x: x.reshape(1, -1)

    tn_merge = _pick(d, (512, 256, 128))
    w_gates = _cast_tiles(w_in, 2 * c + sw, 2 * d, tn_merge)
    zeros = jnp.zeros((dr, c), F32)
    wda = jnp.concatenate([jnp.concatenate([w2, zeros], axis=1),
                           jnp.concatenate([zeros, a2], axis=1)], axis=0).astype(BF16)
    idx = jnp.arange(LANES) // HEAD
    bd = (idx[:, None] == idx[None, :]).astype(BF16)

    x_p = x_prompt.reshape(mp, d)
    x_s = x_sample.reshape(ms, d)
    h = _rms_cast(x_p, x_s, row(ln_mix_pre), min(tm, 256))

    conv_args = (conv_w, row(conv_b), row(conv_ln_g), row(conv_ln_b))
    u = _mm_pair(_mm_glu_kernel, h, w_in, 0, w_in, c, c, F32, "mm_glu")
    u_s = u[mp:].reshape(bs, ts, c)
    zc_p = _conv_seq(u, bp, tp, *conv_args)
    ext_s = jnp.concatenate([conv0, u_s], axis=1)
    zc_s = _conv_step(ext_s.transpose(1, 0, 2), *conv_args).transpose(1, 0, 2).reshape(ms, c)
    conv_p = jnp.stack([u[(q + 1) * tp - (kw - 1):(q + 1) * tp] for q in range(bp)])
    conv_s = ext_s[:, ts:]

    pr = _mm(h, w_in, 3 * c, 2 * c, name="mm_shift")
    pr_low = _mm(h, w_in, sw - 3 * c, 5 * c, name="mm_shift_low")
    pr_s = jnp.concatenate([pr[mp:], pr_low[mp:]], axis=1).reshape(bs, ts, sw)
    shifted_s = jnp.concatenate([shift0[:, None], pr_s[:, :-1]], axis=1).reshape(ms, sw)
    g2_bf = g2.astype(BF16)
    consts = (row(shift_mu), row(w0), row(a0), wda, g2_bf, row(k_k), row(k_a), row(r_k), bd)
    yr_p, st_p = _rwkv_chunk(pr, pr_low, bp, tp, jnp.zeros((bp, c // LANES, HEAD, LANES), F32), row(shift_mu),
                             row(w0), row(a0), wda, g2_bf, row(k_k), row(k_a), row(r_k),
                             row(gn_g), row(gn_b), bd, c, dr)
    if bs % LANES == 0:
        pr_tm = pr_s.transpose(1, 0, 2)
        shifted_tm = jnp.concatenate([shift0[None], pr_tm[:-1]], axis=0)
        vec_s = _rwkv_prep(pr_tm.reshape(ms, sw), shifted_tm.reshape(ms, sw), consts, c, dr)
        y_s, st_t = _wkv_lanes(*vec_s[:6], wkv0.transpose(1, 2, 3, 0), ts)
        yr_tm = _rwkv_post(y_s, vec_s[7], vec_s[3], vec_s[6], row(gn_g), row(gn_b), bd)
        yr_s = yr_tm.reshape(ts, bs, c).transpose(1, 0, 2).reshape(ms, c)
        st_s = st_t.transpose(3, 0, 1, 2)
    else:
        vec_s = _rwkv_prep(pr_s.reshape(ms, sw), shifted_s, consts, c, dr)
        y_s, st_s = _wkv_step(*vec_s[:6], wkv0, bd, ts)
        yr_s = _rwkv_post(y_s, vec_s[7], vec_s[3], vec_s[6], row(gn_g), row(gn_b), bd)
    shift_p = jnp.concatenate(
        [jnp.concatenate([x[(q + 1) * tp - 1:(q + 1) * tp] for x in (pr, pr_low)], axis=1) for q in range(bp)],
        axis=0)
    shift_s = pr_s[:, -1]

    mixed = _merge(h, zc_p, zc_s, yr_p, yr_s, w_gates, _cast_tiles(w_conv_out, 0, d, tn_merge),
                   _cast_tiles(w_rwkv_out, 0, d, tn_merge), b_gate[0:1], b_gate[1:2], tm)
    o = _mm(mixed, w_o, out_dtype=BF16, name="mm_o")
    x1, h2 = _rms_res(x_p, x_s, o, row(ln_mix_post), row(ln_ffn_pre), min(tm, 256))

    act = _mm_pair(_mm_swiglu_kernel, h2, w_ffn_gate, 0, w_ffn_up, 0, w_ffn_gate.shape[1], BF16, "mm_swiglu")
    f = _mm_tiled(act, _cast_tiles(w_ffn_down, 0, d, _pick(d, (512, 256, 128))), BF16, "mm_down")
    tl = min(tm, 256)
    y_prompt = _rms_res_last(x1, f, row(ln_ffn_post), 0, mp, tl)
    y_sample = _rms_res_last(x1, f, row(ln_ffn_post), mp, ms, tl)

    return (y_prompt.reshape(bp, tp, d), y_sample.reshape(bs, ts, d),
            _unpack_state(st_p), conv_p, shift_p,
            st_s, conv_s, shift_s)


def kernel(x_prompt, x_sample, state_wkv, state_conv, state_shift, ln_mix_pre, ln_mix_post, ln_ffn_pre,
           ln_ffn_post, w_in, b_gate, conv_w, conv_b, conv_ln_g, conv_ln_b, w_conv_out, shift_mu, w0, w2,
           a0, a2, g2, k_k, k_a, r_k, gn_g, gn_b, w_rwkv_out, w_o, w_ffn_gate, w_ffn_up, w_ffn_down):
    depth = w_in.shape[0]
    assert depth == 1, "one decoder layer per step"
    weights = (ln_mix_pre, ln_mix_post, ln_ffn_pre, ln_ffn_post, w_in, b_gate,
               conv_w, conv_b, conv_ln_g, conv_ln_b, w_conv_out, shift_mu,
               w0, w2, a0, a2, g2, k_k, k_a, r_k, gn_g, gn_b, w_rwkv_out, w_o,
               w_ffn_gate, w_ffn_up, w_ffn_down)
    lw = tuple(wt[0] for wt in weights)
    yp, ys, wkv_p, conv_p, shift_p, wkv_s, conv_s, shift_s = _layer(
        x_prompt, x_sample, state_wkv[0], state_conv[0], state_shift[0], *lw)
    return (yp, ys, wkv_p[None], conv_p[None], shift_p[None],
            wkv_s[None], conv_s[None], shift_s[None])
```

```python
import functools
import math

import jax
import jax.numpy as jnp
from jax import lax
from jax.experimental import pallas as pl
from jax.experimental.pallas import tpu as pltpu

F32 = jnp.float32
BF16 = jnp.bfloat16

RMS_EPS = 1e-6
LN_EPS = 1e-5
GN_EPS = 64e-5
HEAD = 64
LANES = 128
SUBLANES = 8
CONV_HALO = 32
CHUNK = 64
SPLIT_FROM_LEVEL = 3
VMEM_LIMIT = 56 * 1024 * 1024


def _pick(n, cands):
    for c in cands:
        if n % c == 0:
            return c
    return n


def _params(sem):
    return pltpu.CompilerParams(dimension_semantics=sem, vmem_limit_bytes=VMEM_LIMIT)


def _dot(a, b):
    return jnp.dot(a, b, preferred_element_type=F32)


def _split(x):
    hi = x.astype(BF16)
    return hi, (x - hi.astype(F32)).astype(BF16)


def _mm_split_lhs(a, b):
    hi, lo = _split(a)
    bb = b.astype(BF16)
    return _dot(hi, bb) + _dot(lo, bb)


def _mm_split_both(a, b):
    ah, al = _split(a)
    bh, bl = _split(b)
    return _dot(ah, bh) + _dot(al, bh) + _dot(ah, bl)


def _head_sum(x, bd):
    cols = [_dot(x[:, c:c + LANES].astype(BF16), bd) for c in range(0, x.shape[1], LANES)]
    return jnp.concatenate(cols, axis=1)


def _two_group_specs(tm, d, nblk_p):
    return (pl.BlockSpec((tm, d), lambda i, *_: (jnp.minimum(i, nblk_p - 1), 0)),
            pl.BlockSpec((tm, d), lambda i, *_: (jnp.maximum(i - nblk_p, 0), 0)))


def _two_group_rows(p_ref, s_ref, nblk_p):
    return jnp.where(pl.program_id(0) < nblk_p, p_ref[...], s_ref[...])


def _rms_cast_kernel(xp_ref, xs_ref, g_ref, o_ref, *, nblk_p):
    x = _two_group_rows(xp_ref, xs_ref, nblk_p)
    ms = jnp.mean(x * x, axis=-1, keepdims=True)
    o_ref[...] = (x * lax.rsqrt(ms + RMS_EPS) * g_ref[...]).astype(o_ref.dtype)


def _rms_cast(x_p, x_s, g, tm):
    (mp, d), ms = x_p.shape, x_s.shape[0]
    xp_spec, xs_spec = _two_group_specs(tm, d, mp // tm)
    return pl.pallas_call(
        functools.partial(_rms_cast_kernel, nblk_p=mp // tm),
        grid=((mp + ms) // tm,),
        in_specs=[xp_spec, xs_spec, pl.BlockSpec((1, d), lambda i: (0, 0))],
        out_specs=pl.BlockSpec((tm, d), lambda i: (i, 0)),
        out_shape=jax.ShapeDtypeStruct((mp + ms, d), BF16),
        compiler_params=_params(("parallel",)),
        name="rms_cast",
    )(x_p, x_s, g)


def _rms_res_kernel(xp_ref, xs_ref, o_ref, g_ref, g2_ref, x1_ref, h_ref, *, nblk_p):
    o = o_ref[...].astype(F32)
    ms = jnp.mean(o * o, axis=-1, keepdims=True)
    x1 = _two_group_rows(xp_ref, xs_ref, nblk_p) + o * lax.rsqrt(ms + RMS_EPS) * g_ref[...]
    x1_ref[...] = x1
    ms1 = jnp.mean(x1 * x1, axis=-1, keepdims=True)
    h_ref[...] = (x1 * lax.rsqrt(ms1 + RMS_EPS) * g2_ref[...]).astype(h_ref.dtype)


def _rms_res(x_p, x_s, o, g, g2, tm):
    (mp, d), m = x_p.shape, o.shape[0]
    xp_spec, xs_spec = _two_group_specs(tm, d, mp // tm)
    row = pl.BlockSpec((tm, d), lambda i: (i, 0))
    vec = pl.BlockSpec((1, d), lambda i: (0, 0))
    return pl.pallas_call(
        functools.partial(_rms_res_kernel, nblk_p=mp // tm),
        grid=(m // tm,),
        in_specs=[xp_spec, xs_spec, row, vec, vec],
        out_specs=[row, row],
        out_shape=[jax.ShapeDtypeStruct((m, d), F32), jax.ShapeDtypeStruct((m, d), BF16)],
        compiler_params=_params(("parallel",)),
        name="rms_res",
    )(x_p, x_s, o, g, g2)


def _rms_res_last_kernel(x_ref, o_ref, g_ref, x1_ref):
    o = o_ref[...].astype(F32)
    ms = jnp.mean(o * o, axis=-1, keepdims=True)
    x1_ref[...] = x_ref[...] + o * lax.rsqrt(ms + RMS_EPS) * g_ref[...]


def _rms_res_last(x, o, g, row0, rows, tm):
    d = x.shape[1]
    blk0 = row0 // tm
    src = pl.BlockSpec((tm, d), lambda i: (blk0 + i, 0))
    return pl.pallas_call(
        _rms_res_last_kernel,
        grid=(rows // tm,),
        in_specs=[src, src, pl.BlockSpec((1, d), lambda i: (0, 0))],
        out_specs=pl.BlockSpec((tm, d), lambda i: (i, 0)),
        out_shape=jax.ShapeDtypeStruct((rows, d), F32),
        compiler_params=_params(("parallel",)),
        name="rms_res_last",
    )(x, o, g)


MM_VMEM_BUDGET = 50 * 1024 * 1024


def _mm_tiles(m, k, n, col0s, w_bytes, n_w, out_bytes):
    span = math.gcd(n, *col0s)
    tns = [t for t in (512, 256, 128) if span % t == 0] or [span]
    for tm in (1088, 1024, 544, 512, 256, 128, 64, 32, 16, 8):
        for tn in tns:
            need = (2 * tm * k * 2
                    + n_w * k * tn * (2 * w_bytes + (2 if w_bytes > 2 else 0))
                    + tm * tn * (2 * out_bytes + 4 * n_w))
            if m % tm == 0 and need <= MM_VMEM_BUDGET:
                return tm, tn
    return m, tns[-1]


def _mm_kernel(x_ref, w_ref, o_ref):
    o_ref[...] = _dot(x_ref[...], w_ref[...].astype(BF16)).astype(o_ref.dtype)


def _mm(x, w, n=None, col0=0, out_dtype=F32, name="mm"):
    m, k = x.shape
    n = w.shape[1] if n is None else n
    tm, tn = _mm_tiles(m, k, n, (col0,), w.dtype.itemsize, 1, jnp.dtype(out_dtype).itemsize)
    j0 = col0 // tn
    return pl.pallas_call(
        _mm_kernel,
        grid=(m // tm, n // tn),
        in_specs=[pl.BlockSpec((tm, k), lambda i, j: (i, 0)),
                  pl.BlockSpec((k, tn), lambda i, j: (0, j0 + j))],
        out_specs=pl.BlockSpec((tm, tn), lambda i, j: (i, j)),
        out_shape=jax.ShapeDtypeStruct((m, n), out_dtype),
        compiler_params=_params(("parallel", "arbitrary")),
        name=name,
    )(x, w)


def _mm_glu_kernel(x_ref, wa_ref, wb_ref, o_ref):
    x = x_ref[...]
    o_ref[...] = _dot(x, wa_ref[...].astype(BF16)) * jax.nn.sigmoid(_dot(x, wb_ref[...].astype(BF16)))


def _mm_swiglu_kernel(x_ref, wg_ref, wu_ref, o_ref):
    x = x_ref[...]
    o_ref[...] = (jax.nn.silu(_dot(x, wg_ref[...].astype(BF16)))
                  * _dot(x, wu_ref[...].astype(BF16))).astype(o_ref.dtype)


def _mm_pair(kern, x, wa, col_a, wb, col_b, n, out_dtype, name):
    m, k = x.shape
    tm, tn = _mm_tiles(m, k, n, (col_a, col_b), wa.dtype.itemsize, 2, jnp.dtype(out_dtype).itemsize)
    ja, jb = col_a // tn, col_b // tn
    return pl.pallas_call(
        kern,
        grid=(m // tm, n // tn),
        in_specs=[pl.BlockSpec((tm, k), lambda i, j: (i, 0)),
                  pl.BlockSpec((k, tn), lambda i, j: (0, ja + j)),
                  pl.BlockSpec((k, tn), lambda i, j: (0, jb + j))],
        out_specs=pl.BlockSpec((tm, tn), lambda i, j: (i, j)),
        out_shape=jax.ShapeDtypeStruct((m, n), out_dtype),
        compiler_params=_params(("parallel", "arbitrary")),
        name=name,
    )(x, wa, wb)


def _merge_kernel(h_ref, zcp_ref, zcs_ref, yrp_ref, yrs_ref, wg0_ref, wg1_ref, wc_ref, wr_ref,
                  b0_ref, b1_ref, o_ref, *, nblk_p):
    h = h_ref[...]
    g0 = jax.nn.sigmoid(_dot(h, wg0_ref[0]) + b0_ref[...])
    g1 = jax.nn.sigmoid(_dot(h, wg1_ref[0]) + b1_ref[...])
    oc = _dot(_two_group_rows(zcp_ref, zcs_ref, nblk_p), wc_ref[0])
    orr = _dot(_two_group_rows(yrp_ref, yrs_ref, nblk_p), wr_ref[0])
    o_ref[...] = (g0 * oc + g1 * orr).astype(o_ref.dtype)


def _cast_kernel(*refs):
    *w_refs, o_ref = refs
    sb = w_refs[0].shape[1]
    for q, w_ref in enumerate(w_refs):
        o_ref[0, :, q * sb:(q + 1) * sb] = w_ref[...].astype(o_ref.dtype)


def _cast_tiles(w, col0, n, tn):
    k = w.shape[0]
    sb = math.gcd(col0, tn)
    parts = tn // sb
    assert n % tn == 0 and sb % LANES == 0
    b0 = col0 // sb
    tk = _pick(k, (4096, 2752, 2048, 1024, 512, 256, 128, 64, 32, 16))
    return pl.pallas_call(
        _cast_kernel,
        grid=(n // tn, k // tk),
        in_specs=[pl.BlockSpec((tk, sb), lambda j, kb, q=q: (kb, b0 + parts * j + q)) for q in range(parts)],
        out_specs=pl.BlockSpec((1, tk, tn), lambda j, kb: (j, kb, 0)),
        out_shape=jax.ShapeDtypeStruct((n // tn, k, tn), BF16),
        compiler_params=_params(("parallel", "parallel")),
        name="cast_tiles",
    )(*([w] * parts))


def _mm_tiled_kernel(x_ref, w_ref, o_ref):
    o_ref[...] = _dot(x_ref[...], w_ref[0]).astype(o_ref.dtype)


def _mm_tiled(x, wt, out_dtype, name):
    m, k = x.shape
    nt, _, tn = wt.shape
    out_bytes = jnp.dtype(out_dtype).itemsize
    tm = next(t for t in (1088, 1024, 544, 512, 256, 128, 64, 32, 16, 8, m) if m % t == 0
              and 2 * t * k * 2 + 2 * k * tn * 2 + t * tn * (2 * out_bytes + 4) <= MM_VMEM_BUDGET)
    return pl.pallas_call(
        _mm_tiled_kernel,
        grid=(m // tm, nt),
        in_specs=[pl.BlockSpec((tm, k), lambda i, j: (i, 0)),
                  pl.BlockSpec((1, k, tn), lambda i, j: (j, 0, 0))],
        out_specs=pl.BlockSpec((tm, tn), lambda i, j: (i, j)),
        out_shape=jax.ShapeDtypeStruct((m, nt * tn), out_dtype),
        compiler_params=_params(("parallel", "arbitrary")),
        name=name,
    )(x, wt)


def _merge(h, zc_p, zc_s, yr_p, yr_s, wg, wc, wr, b0, b1, tm):
    m, d = h.shape
    mp, c = zc_p.shape
    tn = wg.shape[2]
    gp_spec, gs_spec = _two_group_specs(tm, c, mp // tm)
    return pl.pallas_call(
        functools.partial(_merge_kernel, nblk_p=mp // tm),
        grid=(m // tm, d // tn),
        in_specs=[pl.BlockSpec((tm, d), lambda i, j: (i, 0)),
                  gp_spec, gs_spec, gp_spec, gs_spec,
                  pl.BlockSpec((1, d, tn), lambda i, j: (j, 0, 0)),
                  pl.BlockSpec((1, d, tn), lambda i, j: (d // tn + j, 0, 0)),
                  pl.BlockSpec((1, c, tn), lambda i, j: (j, 0, 0)),
                  pl.BlockSpec((1, c, tn), lambda i, j: (j, 0, 0)),
                  pl.BlockSpec((1, tn), lambda i, j: (0, j)),
                  pl.BlockSpec((1, tn), lambda i, j: (0, j))],
        out_specs=pl.BlockSpec((tm, tn), lambda i, j: (i, j)),
        out_shape=jax.ShapeDtypeStruct((m, d), BF16),
        compiler_params=_params(("parallel", "arbitrary")),
        name="merge",
    )(h, zc_p, zc_s, yr_p, yr_s, wg, wg, wc, wr, b0, b1)


def _ln_silu(z, g, b):
    mu = jnp.mean(z, axis=-1, keepdims=True)
    zc = z - mu
    var = jnp.mean(zc * zc, axis=-1, keepdims=True)
    y = zc * lax.rsqrt(var + LN_EPS) * g + b
    return y * jax.nn.sigmoid(y)


def _conv_seq_kernel(um_ref, uh_ref, w_ref, cb_ref, g_ref, b_ref, o_ref, x_scr, xs_scr, z_scr,
                     *, tb, kw, lc):
    i = pl.program_id(1)
    c = um_ref.shape[1]
    halo = uh_ref[...]
    x_scr[0:CONV_HALO, :] = jnp.where(i == 0, jnp.zeros_like(halo), halo)
    x_scr[CONV_HALO:CONV_HALO + tb, :] = um_ref[...]
    off = CONV_HALO - (kw - 1)
    first = True
    for s in range(SUBLANES):
        taps = [j for j in range(kw) if (off + j) % SUBLANES == s]
        if not taps:
            continue
        span = max(off + j - s for j in taps) + tb
        xs_scr[0:span, :] = x_scr[s:s + span, :]
        for c0 in range(0, c, lc):
            cols = slice(c0, c0 + lc)
            acc = cb_ref[:, cols] if first else z_scr[:, cols]
            for j in taps:
                a0 = off + j - s
                acc = acc + xs_scr[a0:a0 + tb, cols] * w_ref[j:j + 1, cols]
            z_scr[:, cols] = acc
        first = False
    o_ref[...] = _ln_silu(z_scr[...], g_ref[...], b_ref[...]).astype(o_ref.dtype)


def _conv_seq(u, nseq, t, conv_w, conv_b, ln_g, ln_b):
    c = u.shape[1]
    kw = conv_w.shape[0]
    tb = _pick(t, (128, 64, 32))
    lc = _pick(c, (128,))
    nb = t // tb
    hb = tb // CONV_HALO
    vec = pl.BlockSpec((1, c), lambda bi, i: (0, 0))
    return pl.pallas_call(
        functools.partial(_conv_seq_kernel, tb=tb, kw=kw, lc=lc),
        grid=(nseq, nb),
        in_specs=[pl.BlockSpec((tb, c), lambda bi, i: (bi * nb + i, 0)),
                  pl.BlockSpec((CONV_HALO, c), lambda bi, i: (jnp.maximum((bi * nb + i) * hb - 1, 0), 0)),
                  pl.BlockSpec((kw, c), lambda bi, i: (0, 0)),
                  vec, vec, vec],
        out_specs=pl.BlockSpec((tb, c), lambda bi, i: (bi * nb + i, 0)),
        out_shape=jax.ShapeDtypeStruct((nseq * t, c), BF16),
        scratch_shapes=[pltpu.VMEM((CONV_HALO + tb, c), F32), pltpu.VMEM((CONV_HALO + tb, c), F32),
                        pltpu.VMEM((tb, c), F32)],
        compiler_params=_params(("parallel", "arbitrary")),
        name="conv_seq",
    )(u, u, conv_w, conv_b, ln_g, ln_b)


def _conv_step_kernel(e_ref, w_ref, cb_ref, g_ref, b_ref, o_ref, *, kw, nt):
    for t in range(nt):
        acc = e_ref[t] * w_ref[0:1, :]
        for j in range(1, kw):
            acc = acc + e_ref[t + j] * w_ref[j:j + 1, :]
        o_ref[t] = _ln_silu(acc + cb_ref[...], g_ref[...], b_ref[...]).astype(o_ref.dtype)


def _conv_step(ext_tm, conv_w, conv_b, ln_g, ln_b):
    te, b, c = ext_tm.shape
    kw = conv_w.shape[0]
    nt = te - (kw - 1)
    sb = _pick(b, (8,))
    vec = pl.BlockSpec((1, c), lambda i: (0, 0))
    return pl.pallas_call(
        functools.partial(_conv_step_kernel, kw=kw, nt=nt),
        grid=(b // sb,),
        in_specs=[pl.BlockSpec((te, sb, c), lambda i: (0, i, 0)),
                  pl.BlockSpec((kw, c), lambda i: (0, 0)),
                  vec, vec, vec],
        out_specs=pl.BlockSpec((nt, sb, c), lambda i: (0, i, 0)),
        out_shape=jax.ShapeDtypeStruct((nt, b, c), BF16),
        compiler_params=_params(("parallel",)),
        name="conv_step",
    )(ext_tm, conv_w, conv_b, ln_g, ln_b)


def _rwkv_vector_math(m_r, m_k, m_v, m_low, w0, a0, wd_w, wd_a, g2, k_k, k_a, r_k, bd, dr):
    low = m_low[:, 0:2 * dr]
    lane = lax.broadcasted_iota(jnp.int32, low.shape, 1)
    low = jnp.where(lane < dr, jnp.tanh(low), low).astype(BF16)
    logw = -math.exp(-0.5) * jax.nn.sigmoid(w0 + _dot(low, wd_w))
    rate = jax.nn.sigmoid(a0 + _dot(low, wd_a))
    gate = _dot(jax.nn.sigmoid(m_low[:, 2 * dr:]).astype(BF16), g2)
    kk = m_k * k_k
    kk = kk / jnp.maximum(jnp.sqrt(_head_sum(kk * kk, bd)), 1e-12)
    k2 = m_k * (1.0 + (rate - 1.0) * k_a)
    bonus = _head_sum(m_r * k2 * r_k, bd)
    return m_r, logw, k2, m_v, -kk, kk * rate, gate, bonus


def _rwkv_prep_kernel(pr_ref, sh_ref, mu_ref, w0_ref, a0_ref, wda_ref, g2_ref, kk_ref, ka_ref, rk_ref,
                      bd_ref, *outs, c, dr):
    pr = pr_ref[...]
    m = pr + (sh_ref[...] - pr) * mu_ref[...]
    vals = _rwkv_vector_math(m[:, 0:c], m[:, c:2 * c], m[:, 2 * c:3 * c], m[:, 3 * c:],
                             w0_ref[...], a0_ref[...], wda_ref[:, 0:c], wda_ref[:, c:2 * c], g2_ref[...],
                             kk_ref[...], ka_ref[...], rk_ref[...], bd_ref[...], dr)
    for o_ref, val in zip(outs, vals):
        o_ref[...] = val


def _rwkv_prep(pr, shifted, consts, c, dr):
    rows, sw = pr.shape
    tm = _pick(rows, (128, 64, 32, 16, 8))
    row = pl.BlockSpec((tm, sw), lambda i: (i, 0))
    out = pl.BlockSpec((tm, c), lambda i: (i, 0))
    full = lambda x: pl.BlockSpec(x.shape, lambda i: (0,) * x.ndim)
    return pl.pallas_call(
        functools.partial(_rwkv_prep_kernel, c=c, dr=dr),
        grid=(rows // tm,),
        in_specs=[row, row] + [full(x) for x in consts],
        out_specs=[out] * 8,
        out_shape=[jax.ShapeDtypeStruct((rows, c), F32)] * 8,
        compiler_params=_params(("parallel",)),
        name="rwkv_prep",
    )(pr, shifted, *consts)


def _wkv_step_kernel(r_ref, w_ref, k_ref, v_ref, a_ref, b_ref, s0_ref, bd_ref, y_ref, st_ref,
                     *, nseq, t, g):
    bd = bd_ref[...]
    rows = lax.broadcasted_iota(jnp.int32, (HEAD, LANES), 0)
    lanes = lax.broadcasted_iota(jnp.int32, (HEAD, LANES), 1)
    diag = jnp.where(rows == lanes % HEAD, 1.0, 0.0).astype(F32)
    chains = [(q, p) for q in range(nseq) for p in range(g)]
    rr, kk, vv, aa, bb = (ref[...] for ref in (r_ref, k_ref, v_ref, a_ref, b_ref))
    ww = jnp.exp(w_ref[...])
    s = [jnp.concatenate([s0_ref[q, 2 * p], s0_ref[q, 2 * p + 1]], axis=1) for q, p in chains]
    yrows = {}
    for i in range(t):
        bc = lambda x, q, p: jnp.broadcast_to(
            x[q * t + i:q * t + i + 1, p * LANES:(p + 1) * LANES], (HEAD, LANES))
        lhs = jnp.concatenate(
            [jnp.concatenate([s[n] * bc(aa, q, p), bc(vv, q, p) * diag], axis=0)
             for n, (q, p) in enumerate(chains)], axis=0)
        res = _dot(lhs.astype(BF16), bd)
        s = [s[n] * bc(ww, q, p) + res[2 * n * HEAD:(2 * n + 1) * HEAD] * bc(bb, q, p)
             + res[(2 * n + 1) * HEAD:(2 * n + 2) * HEAD] * bc(kk, q, p)
             for n, (q, p) in enumerate(chains)]
        sr = jnp.concatenate([s[n] * bc(rr, q, p) for n, (q, p) in enumerate(chains)], axis=0)
        yc = _dot(sr.astype(BF16), bd)
        for n, (q, p) in enumerate(chains):
            yrows[(q, i, p)] = jnp.sum(yc[n * HEAD:(n + 1) * HEAD] * diag, axis=0, keepdims=True)
    y_ref[...] = jnp.concatenate(
        [jnp.concatenate([yrows[(q, i, p)] for p in range(g)], axis=1)
         for q in range(nseq) for i in range(t)], axis=0)
    for n, (q, p) in enumerate(chains):
        st_ref[q, 2 * p] = s[n][:, 0:HEAD]
        st_ref[q, 2 * p + 1] = s[n][:, HEAD:2 * HEAD]


def _wkv_step(r, lw, k, v, a, b, s0, bd, t):
    rows, c = r.shape
    npair = c // LANES
    assert SUBLANES % t == 0, "token-by-token path expects a few new tokens per sequence"
    nseq = SUBLANES // t
    g = _pick(npair, (8, 4, 2, 1))
    seq = pl.BlockSpec((nseq * t, g * LANES), lambda si, pi: (si, pi))
    st = pl.BlockSpec((nseq, 2 * g, HEAD, HEAD), lambda si, pi: (si, pi, 0, 0))
    return pl.pallas_call(
        functools.partial(_wkv_step_kernel, nseq=nseq, t=t, g=g),
        grid=(rows // (nseq * t), npair // g),
        in_specs=[seq] * 6 + [st, pl.BlockSpec((LANES, LANES), lambda si, pi: (0, 0))],
        out_specs=[seq, st],
        out_shape=[jax.ShapeDtypeStruct((rows, c), F32),
                   jax.ShapeDtypeStruct(s0.shape, F32)],
        compiler_params=_params(("parallel", "parallel")),
        name="wkv_step",
    )(r, lw, k, v, a, b, s0, bd)


def _wkv_lanes_kernel(r_ref, w_ref, k_ref, v_ref, a_ref, b_ref, s0_ref, y_ref, st_ref, xt_scr, yt_scr,
                      *, t, nb):
    names = (r_ref, w_ref, k_ref, v_ref, a_ref, b_ref)
    for i, ref in enumerate(names):
        for step in range(t):
            x = ref[step * nb:(step + 1) * nb, :].T
            xt_scr[i, step] = jnp.exp(x) if ref is w_ref else x
    ir, iw, ik, iv, ia, ib = range(6)
    sub = lax.broadcasted_iota(jnp.int32, (SUBLANES, nb), 0)
    for hh in range(2):
        ch = slice(hh * HEAD, (hh + 1) * HEAD)

        def group(vg, carry, hh=hh, ch=ch):
            v0 = pl.multiple_of(vg * SUBLANES, SUBLANES)
            vt = [xt_scr[iv, step, pl.ds(hh * HEAD + v0, SUBLANES), :] for step in range(t)]
            ytile = [jnp.zeros((SUBLANES, nb), F32) for _ in range(t)]
            for vi in range(SUBLANES):
                s = s0_ref[hh, v0 + vi]
                for step in range(t):
                    sa = jnp.sum(s * xt_scr[ia, step, ch, :], axis=0, keepdims=True)
                    s = (s * xt_scr[iw, step, ch, :] + sa * xt_scr[ib, step, ch, :]
                         + vt[step][vi:vi + 1, :] * xt_scr[ik, step, ch, :])
                    yrow = jnp.sum(s * xt_scr[ir, step, ch, :], axis=0, keepdims=True)
                    ytile[step] = jnp.where(sub == vi, yrow, ytile[step])
                st_ref[hh, v0 + vi] = s
            for step in range(t):
                yt_scr[step, pl.ds(hh * HEAD + v0, SUBLANES), :] = ytile[step]
            return carry

        lax.fori_loop(0, HEAD // SUBLANES, group, 0)
    for step in range(t):
        y_ref[step * nb:(step + 1) * nb, :] = yt_scr[step].T


def _wkv_lanes(r, lw, k, v, a, b, s0_t, t):
    rows, c = r.shape
    nb = rows // t
    seq = pl.BlockSpec((rows, LANES), lambda p: (0, p))
    st = pl.BlockSpec((2, HEAD, HEAD, nb), lambda p: (p, 0, 0, 0))
    return pl.pallas_call(
        functools.partial(_wkv_lanes_kernel, t=t, nb=nb),
        grid=(c // LANES,),
        in_specs=[seq] * 6 + [st],
        out_specs=[seq, st],
        out_shape=[jax.ShapeDtypeStruct((rows, c), F32), jax.ShapeDtypeStruct(s0_t.shape, F32)],
        scratch_shapes=[pltpu.VMEM((6, t, LANES, nb), F32), pltpu.VMEM((t, LANES, nb), F32)],
        compiler_params=_params(("parallel",)),
        name="wkv_lanes",
    )(r, lw, k, v, a, b, s0_t)


def _chunk_masks():
    n = 2 * CHUNK
    ri = jnp.arange(n)[:, None]
    ci = jnp.arange(n)[None, :]
    levels = [(ri // 2) == (ci // 2)]
    bsz = 2
    while bsz < CHUNK:
        levels.append(((ri // (2 * bsz)) == (ci // (2 * bsz))) & ((ri // bsz) != (ci // bsz)))
        bsz *= 2
    levels.append(ri == ci)
    lv = jnp.stack(levels).astype(F32)
    r4 = jnp.arange(2 * n)[:, None]
    c4 = jnp.arange(2 * n)[None, :]
    same_head = ((r4 // CHUNK) % 2) == ((c4 // CHUNK) % 2)
    t, s = r4 % CHUNK, c4 % CHUNK
    m1 = jnp.where(r4 < n, same_head & (s < t), same_head & (s <= t)).astype(F32)
    tri = (jnp.arange(CHUNK)[:, None] >= jnp.arange(CHUNK)[None, :]).astype(BF16)
    return lv, m1, tri


def _post_math(y, bonus, v, gate, gn_g, gn_b, bd):
    ym = _head_sum(y, bd) * (1.0 / HEAD)
    yc = y - ym
    yv = _head_sum(yc * yc, bd) * (1.0 / HEAD)
    yn = yc * lax.rsqrt(yv + GN_EPS) * gn_g + gn_b
    return (yn + bonus * v) * gate


def _rwkv_chunk_kernel(pr_r, pr_k, pr_v, pr_l, hr_ref, hk_ref, hv_ref, hl_ref, mu_r, mu_k, mu_v, mu_l,
                       w0_ref, a0_ref, wdw_ref, wda_ref, g2_ref, kk_ref, ka_ref, rk_ref, gg_ref, gb_ref,
                       s0_ref, lv_ref, m1_ref, tri_ref, bd_ref, y_ref, st_ref, *, g, dr):
    ci = pl.program_id(2)
    n = 2 * CHUNK
    lane = lax.broadcasted_iota(jnp.int32, (1, LANES), 1)
    m0 = jnp.where(lane < HEAD, 1.0, 0.0).astype(F32)
    m1 = 1.0 - m0
    expand = lambda x: jnp.concatenate([x * m0, x * m1], axis=0)
    fold = lambda x: x[0:CHUNK] + x[CHUNK:n]

    @pl.when(ci == 0)
    def _():
        st_ref[...] = s0_ref[...]

    def shifted_lerp(x_ref, h_ref, mu_ref):
        x = x_ref[...]
        prev = h_ref[SUBLANES - 1:SUBLANES, :]
        prev = jnp.where(ci == 0, jnp.zeros_like(prev), prev)
        row = lax.broadcasted_iota(jnp.int32, x.shape, 0)
        shifted = jnp.where(row == 0, jnp.broadcast_to(prev, x.shape), pltpu.roll(x, 1, 0))
        return x + (shifted - x) * mu_ref[...]

    bd = bd_ref[...]
    r_all, lw_all, k_all, v_all, a_all, b_all, gate, bonus = _rwkv_vector_math(
        shifted_lerp(pr_r, hr_ref, mu_r), shifted_lerp(pr_k, hk_ref, mu_k), shifted_lerp(pr_v, hv_ref, mu_v),
        shifted_lerp(pr_l, hl_ref, mu_l), w0_ref[...], a0_ref[...], wdw_ref[...], wda_ref[...], g2_ref[...],
        kk_ref[...], ka_ref[...], rk_ref[...], bd, dr)

    pairs = range(g)
    tri = tri_ref[...]
    mask1 = m1_ref[...]
    nlev = lv_ref.shape[0]
    lv = [lv_ref[i] for i in range(nlev)]
    sls = [slice(p * LANES, (p + 1) * LANES) for p in pairs]
    r, lw, k, v, a, b = ([x[:, sl] for sl in sls] for x in (r_all, lw_all, k_all, v_all, a_all, b_all))
    s_old = [expand(st_ref[0, p]) for p in pairs]
    lw_split = [_split(x) for x in lw]
    cum = [_dot(tri, hi) + _dot(tri, lo) for hi, lo in lw_split]
    cum_l = [x[CHUNK - 1:CHUNK] for x in cum]
    e_neg = [jnp.exp(-x) for x in cum]
    e_hat = [jnp.exp(cl - x) for cl, x in zip(cum_l, cum)]
    at_e = [expand(a[p] * jnp.exp(cum[p] - lw[p])) for p in pairs]
    rt_e = [expand(r[p] * jnp.exp(cum[p])) for p in pairs]
    v_e = [expand(x) for x in v]
    bt = [(b[p] * e_neg[p]).astype(BF16) for p in pairs]
    kt = [(k[p] * e_neg[p]).astype(BF16) for p in pairs]
    nt = (((1,), (1,)), ((), ()))
    out1 = [lax.dot_general(jnp.concatenate([at_e[p], rt_e[p]], axis=0).astype(BF16),
                            jnp.concatenate([bt[p], bt[p], kt[p], kt[p]], axis=0), nt,
                            preferred_element_type=F32) for p in pairs]
    out1 = [jnp.where(mask1 != 0.0, x, 0.0) for x in out1]
    a_ab = [x[0:n, 0:n] for x in out1]
    m_r = [x[n:2 * n, :].astype(BF16) for x in out1]
    akv = [_dot(out1[p][0:n, n:2 * n].astype(BF16), v_e[p].astype(BF16)) for p in pairs]
    tm = [lv[nlev - 1] + x * lv[0] for x in a_ab]
    for lev in range(1, nlev - 1):
        mm = _mm_split_lhs if lev > SPLIT_FROM_LEVEL else (lambda x, y: _dot(x.astype(BF16), y.astype(BF16)))
        step = [mm(tm[p], a_ab[p] * lv[lev]) for p in pairs]
        tm = [tm[p] + mm(step[p], tm[p]) for p in pairs]
    wu = [_mm_split_both(tm[p], jnp.concatenate([at_e[p], akv[p]], axis=1)) for p in pairs]
    rhs4 = [jnp.concatenate([wu[p], jnp.concatenate([jnp.zeros_like(v_e[p]), v_e[p]], axis=1)],
                            axis=0).astype(BF16) for p in pairs]
    o4 = [_dot(m_r[p], rhs4[p]) for p in pairs]
    q = [fold(rt_e[p] + o4[p][:, 0:LANES]).astype(BF16) for p in pairs]
    y1 = [fold(o4[p][:, LANES:2 * LANES]) for p in pairs]
    rhs5 = [jnp.concatenate([expand(b[p] * e_hat[p]), expand(k[p] * e_hat[p])], axis=0).astype(BF16)
            for p in pairs]
    o5 = [lax.dot_general(rhs4[p], rhs5[p], (((0,), (0,)), ((), ())), preferred_element_type=F32)
          for p in pairs]
    s_bf = [x.astype(BF16) for x in s_old]
    y = [lax.dot_general(q[p], s_bf[p], nt, preferred_element_type=F32) + y1[p] for p in pairs]
    s_new = [s_old[p] * jnp.exp(cum_l[p]) + _dot(s_bf[p], o5[p][0:n].astype(BF16)) + o5[p][n:2 * n]
             for p in pairs]
    y_all = jnp.concatenate(y, axis=1)
    y_ref[...] = _post_math(y_all, bonus, v_all, gate, gg_ref[...], gb_ref[...], bd).astype(y_ref.dtype)
    for p in pairs:
        st_ref[0, p] = fold(s_new[p])


def _rwkv_chunk(pr, pr_low, nseq, t, s0, mu, w0, a0, wda, g2, k_k, k_a, r_k, gn_g, gn_b, bd, c, dr):
    npair = c // LANES
    nc = t // CHUNK
    g = _pick(npair, (16, 8, 4, 2, 1))
    gw = g * LANES
    lw_ = pr_low.shape[1]
    assert pr.shape[1] == 3 * c and (3 * c) % lw_ == 0 and c % gw == 0
    hb = CHUNK // SUBLANES
    lv, m1, tri = _chunk_masks()
    row_blk = lambda si, ci: si * nc + ci
    halo_blk = lambda si, ci: jnp.maximum((si * nc + ci) * hb - 1, 0)
    wide = lambda part: pl.BlockSpec((CHUNK, gw), lambda si, pi, ci: (row_blk(si, ci), part * (c // gw) + pi))
    wide_h = lambda part: pl.BlockSpec((SUBLANES, gw),
                                       lambda si, pi, ci: (halo_blk(si, ci), part * (c // gw) + pi))
    low = pl.BlockSpec((CHUNK, lw_), lambda si, pi, ci: (row_blk(si, ci), 0))
    low_h = pl.BlockSpec((SUBLANES, lw_), lambda si, pi, ci: (halo_blk(si, ci), 0))
    vec = lambda part: pl.BlockSpec((1, gw), lambda si, pi, ci: (0, part * (c // gw) + pi))
    vec_low = pl.BlockSpec((1, lw_), lambda si, pi, ci: (0, 3 * c // lw_))
    cols = lambda rows, part: pl.BlockSpec((rows, gw), lambda si, pi, ci: (0, part * (c // gw) + pi))
    st = pl.BlockSpec((1, g, HEAD, LANES), lambda si, pi, ci: (si, pi, 0, 0))
    const = lambda x: pl.BlockSpec(x.shape, lambda si, pi, ci: (0,) * x.ndim)
    return pl.pallas_call(
        functools.partial(_rwkv_chunk_kernel, g=g, dr=dr),
        grid=(nseq, npair // g, nc),
        in_specs=[wide(0), wide(1), wide(2), low, wide_h(0), wide_h(1), wide_h(2), low_h,
                  vec(0), vec(1), vec(2), vec_low,
                  vec(0), vec(0), cols(2 * dr, 0), cols(2 * dr, 1), cols(g2.shape[0], 0),
                  vec(0), vec(0), vec(0), vec(0), vec(0),
                  st, const(lv), const(m1), const(tri), const(bd)],
        out_specs=[pl.BlockSpec((CHUNK, gw), lambda si, pi, ci: (row_blk(si, ci), pi)), st],
        out_shape=[jax.ShapeDtypeStruct((nseq * t, c), BF16),
                   jax.ShapeDtypeStruct((nseq, npair, HEAD, LANES), F32)],
        compiler_params=_params(("parallel", "parallel", "arbitrary")),
        name="rwkv_chunk",
    )(pr, pr, pr, pr_low, pr, pr, pr, pr_low, mu, mu, mu, mu, w0, a0, wda, wda, g2, k_k, k_a, r_k, gn_g, gn_b,
      s0, lv, m1, tri, bd)


def _rwkv_post_kernel(y_ref, bn_ref, v_ref, g_ref, gg_ref, gb_ref, bd_ref, o_ref):
    o_ref[...] = _post_math(y_ref[...], bn_ref[...], v_ref[...], g_ref[...], gg_ref[...], gb_ref[...],
                            bd_ref[...]).astype(o_ref.dtype)


def _rwkv_post(y, bn, v, g, gn_g, gn_b, bd):
    m, c = y.shape
    tm = _pick(m, (256, 128, 64, 32, 16, 8))
    row = pl.BlockSpec((tm, c), lambda i: (i, 0))
    vec = pl.BlockSpec((1, c), lambda i: (0, 0))
    return pl.pallas_call(
        _rwkv_post_kernel,
        grid=(m // tm,),
        in_specs=[row, row, row, row, vec, vec, pl.BlockSpec((LANES, LANES), lambda i: (0, 0))],
        out_specs=row,
        out_shape=jax.ShapeDtypeStruct((m, c), BF16),
        compiler_params=_params(("parallel",)),
        name="rwkv_post",
    )(y, bn, v, g, gn_g, gn_b, bd)


def _unpack_state(s):
    n, hp = s.shape[0], s.shape[1]
    return s.reshape(n, hp, HEAD, 2, HEAD).transpose(0, 1, 3, 2, 4).reshape(n, 2 * hp, HEAD, HEAD)


def _layer(x_prompt, x_sample, wkv0, conv0, shift0,
           ln_mix_pre, ln_mix_post, ln_ffn_pre, ln_ffn_post, w_in, b_gate,
           conv_w, conv_b, conv_ln_g, conv_ln_b, w_conv_out, shift_mu,
           w0, w2, a0, a2, g2, k_k, k_a, r_k, gn_g, gn_b, w_rwkv_out, w_o,
           w_ffn_gate, w_ffn_up, w_ffn_down):
    bp, tp, d = x_prompt.shape
    bs, ts, _ = x_sample.shape
    c = conv_w.shape[1]
    kw = conv_w.shape[0]
    dr = w2.shape[0]
    sw = shift_mu.shape[0]
    mp, ms = bp * tp, bs * ts
    assert tp % CHUNK == 0 and tp >= kw - 1, "prompt sequences are processed in 64-token chunks"
    tm = _pick(math.gcd(mp, ms), (512, 256, 128, 64, 32, 16))
    row = lambda x: x.reshape(1, -1)

    tn_merge = _pick(d, (512, 256, 128))
    w_gates = _cast_tiles(w_in, 2 * c + sw, 2 * d, tn_merge)
    zeros = jnp.zeros((dr, c), F32)
    wda = jnp.concatenate([jnp.concatenate([w2, zeros], axis=1),
                           jnp.concatenate([zeros, a2], axis=1)], axis=0).astype(BF16)
    idx = jnp.arange(LANES) // HEAD
    bd = (idx[:, None] == idx[None, :]).astype(BF16)

    x_p = x_prompt.reshape(mp, d)
    x_s = x_sample.reshape(ms, d)
    h = _rms_cast(x_p, x_s, row(ln_mix_pre), min(tm, 256))

    conv_args = (conv_w, row(conv_b), row(conv_ln_g), row(conv_ln_b))
    u = _mm_pair(_mm_glu_kernel, h, w_in, 0, w_in, c, c, F32, "mm_glu")
    u_s = u[mp:].reshape(bs, ts, c)
    zc_p = _conv_seq(u, bp, tp, *conv_args)
    ext_s = jnp.concatenate([conv0, u_s], axis=1)
    zc_s = _conv_step(ext_s.transpose(1, 0, 2), *conv_args).transpose(1, 0, 2).reshape(ms, c)
    conv_p = jnp.stack([u[(q + 1) * tp - (kw - 1):(q + 1) * tp] for q in range(bp)])
    conv_s = ext_s[:, ts:]

    pr = _mm(h, w_in, 3 * c, 2 * c, name="mm_shift")
    pr_low = _mm(h, w_in, sw - 3 * c, 5 * c, name="mm_shift_low")
    pr_s = jnp.concatenate([pr[mp:], pr_low[mp:]], axis=1).reshape(bs, ts, sw)
    shifted_s = jnp.concatenate([shift0[:, None], pr_s[:, :-1]], axis=1).reshape(ms, sw)
    g2_bf = g2.astype(BF16)
    consts = (row(shift_mu), row(w0), row(a0), wda, g2_bf, row(k_k), row(k_a), row(r_k), bd)
    yr_p, st_p = _rwkv_chunk(pr, pr_low, bp, tp, jnp.zeros((bp, c // LANES, HEAD, LANES), F32), row(shift_mu),
                             row(w0), row(a0), wda, g2_bf, row(k_k), row(k_a), row(r_k),
                             row(gn_g), row(gn_b), bd, c, dr)
    if bs % LANES == 0:
        pr_tm = pr_s.transpose(1, 0, 2)
        shifted_tm = jnp.concatenate([shift0[None], pr_tm[:-1]], axis=0)
        vec_s = _rwkv_prep(pr_tm.reshape(ms, sw), shifted_tm.reshape(ms, sw), consts, c, dr)
        y_s, st_t = _wkv_lanes(*vec_s[:6], wkv0.transpose(1, 2, 3, 0), ts)
        yr_tm = _rwkv_post(y_s, vec_s[7], vec_s[3], vec_s[6], row(gn_g), row(gn_b), bd)
        yr_s = yr_tm.reshape(ts, bs, c).transpose(1, 0, 2).reshape(ms, c)
        st_s = st_t.transpose(3, 0, 1, 2)
    else:
        vec_s = _rwkv_prep(pr_s.reshape(ms, sw), shifted_s, consts, c, dr)
        y_s, st_s = _wkv_step(*vec_s[:6], wkv0, bd, ts)
        yr_s = _rwkv_post(y_s, vec_s[7], vec_s[3], vec_s[6], row(gn_g), row(gn_b), bd)
    shift_p = jnp.concatenate(
        [jnp.concatenate([x[(q + 1) * tp - 1:(q + 1) * tp] for x in (pr, pr_low)], axis=1) for q in range(bp)],
        axis=0)
    shift_s = pr_s[:, -1]

    mixed = _merge(h, zc_p, zc_s, yr_p, yr_s, w_gates, _cast_tiles(w_conv_out, 0, d, tn_merge),
                   _cast_tiles(w_rwkv_out, 0, d, tn_merge), b_gate[0:1], b_gate[1:2], tm)
    o = _mm(mixed, w_o, out_dtype=BF16, name="mm_o")
    x1, h2 = _rms_res(x_p, x_s, o, row(ln_mix_post), row(ln_ffn_pre), min(tm, 256))

    act = _mm_pair(_mm_swiglu_kernel, h2, w_ffn_gate, 0, w_ffn_up, 0, w_ffn_gate.shape[1], BF16, "mm_swiglu")
    f = _mm_tiled(act, _cast_tiles(w_ffn_down, 0, d, _pick(d, (512, 256, 128))), BF16, "mm_down")
    tl = min(tm, 256)
    y_prompt = _rms_res_last(x1, f, row(ln_ffn_post), 0, mp, tl)
    y_sample = _rms_res_last(x1, f, row(ln_ffn_post), mp, ms, tl)

    return (y_prompt.reshape(bp, tp, d), y_sample.reshape(bs, ts, d),
            _unpack_state(st_p), conv_p, shift_p,
            st_s, conv_s, shift_s)


def kernel(x_prompt, x_sample, state_wkv, state_conv, state_shift, ln_mix_pre, ln_mix_post, ln_ffn_pre,
           ln_ffn_post, w_in, b_gate, conv_w, conv_b, conv_ln_g, conv_ln_b, w_conv_out, shift_mu, w0, w2,
           a0, a2, g2, k_k, k_a, r_k, gn_g, gn_b, w_rwkv_out, w_o, w_ffn_gate, w_ffn_up, w_ffn_down):
    depth = w_in.shape[0]
    assert depth == 1, "one decoder layer per step"
    weights = (ln_mix_pre, ln_mix_post, ln_ffn_pre, ln_ffn_post, w_in, b_gate,
               conv_w, conv_b, conv_ln_g, conv_ln_b, w_conv_out, shift_mu,
               w0, w2, a0, a2, g2, k_k, k_a, r_k, gn_g, gn_b, w_rwkv_out, w_o,
               w_ffn_gate, w_ffn_up, w_ffn_down)
    lw = tuple(wt[0] for wt in weights)
    yp, ys, wkv_p, conv_p, shift_p, wkv_s, conv_s, shift_s = _layer(
        x_prompt, x_sample, state_wkv[0], state_conv[0], state_shift[0], *lw)
    return (yp, ys, wkv_p[None], conv_p[None], shift_p[None],
            wkv_s[None], conv_s[None], shift_s[None])
```
